```python
import math
import jax, jax.numpy as jnp
from jax import lax
import numpy as np

D_MODEL = 2048
BATCH = 4
SEQ = 4096
DEPTH = 2

N_MIXERS = 2
N_MLA_LAYERS = (DEPTH + 1) // 2
N_GDN_LAYERS = DEPTH // 2
RMS_EPS = 1e-6

MLA_HEADS = 16
MLA_Q_RANK = 512
MLA_KV_RANK = 512
MLA_NOPE_DIM = 128
MLA_ROPE_DIM = 64
MLA_V_DIM = 128
MLA_QK_DIM = MLA_NOPE_DIM + MLA_ROPE_DIM
MLA_IN_DIM = MLA_Q_RANK + MLA_KV_RANK + MLA_ROPE_DIM
ROPE_THETA = 10000.0
Q_BLOCK = 128

GDN_QK_HEADS = 16
GDN_V_HEADS = 32
GDN_HEAD_DIM_K = 128
GDN_HEAD_DIM_V = 128
GDN_KEY_DIM = GDN_QK_HEADS * GDN_HEAD_DIM_K
GDN_VALUE_DIM = GDN_V_HEADS * GDN_HEAD_DIM_V
GDN_CONV_DIM = 2 * GDN_KEY_DIM + GDN_VALUE_DIM
GDN_IN_DIM = GDN_CONV_DIM + GDN_VALUE_DIM + 2 * GDN_V_HEADS
GDN_CONV = 4
GDN_CHUNK = 64

D_FF = 5632
FFN_CONV = 3

kernel_name = "hybrid_mla_gdn_convffn"


def rmsnorm(x, w):
    xf = x.astype(jnp.float32)
    y = xf * lax.rsqrt(jnp.mean(xf * xf, axis=-1, keepdims=True) + RMS_EPS)
    return (y * w.astype(jnp.float32)).astype(x.dtype)


def l2norm(t):
    return t * lax.rsqrt(jnp.sum(t * t, axis=-1, keepdims=True) + 1e-6)


def causal_dwconv(x, w):
    width, ch = w.shape
    return lax.conv_general_dilated(
        x, w[:, None, :].astype(x.dtype), window_strides=(1,),
        padding=[(width - 1, 0)], dimension_numbers=("NWC", "WIO", "NWC"),
        feature_group_count=ch)


def rope_tables(positions, dim):
    inv_freq = ROPE_THETA ** (-jnp.arange(0, dim, 2, dtype=jnp.float32) / dim)
    ang = positions.astype(jnp.float32)[..., None] * inv_freq
    return jnp.cos(ang), jnp.sin(ang)


def apply_rope(t, cos, sin):
    t1, t2 = jnp.split(t.astype(jnp.float32), 2, axis=-1)
    return jnp.concatenate([t1 * cos - t2 * sin, t2 * cos + t1 * sin], axis=-1).astype(t.dtype)


def mla_mixer(h, positions, w_in, q_norm, kv_norm, w_uq, w_ukv, w_o):
    B, S, _ = h.shape
    proj = h @ w_in
    c_q, c_kv, k_rope = jnp.split(proj, [MLA_Q_RANK, MLA_Q_RANK + MLA_KV_RANK], axis=-1)
    q = (rmsnorm(c_q, q_norm) @ w_uq).reshape(B, S, MLA_HEADS, MLA_QK_DIM)
    q_nope, q_rope = q[..., :MLA_NOPE_DIM], q[..., MLA_NOPE_DIM:]
    kv = (rmsnorm(c_kv, kv_norm) @ w_ukv).reshape(B, S, MLA_HEADS, MLA_NOPE_DIM + MLA_V_DIM)
    k_nope, v = kv[..., :MLA_NOPE_DIM], kv[..., MLA_NOPE_DIM:]
    cos, sin = rope_tables(positions, MLA_ROPE_DIM)
    q_rope = apply_rope(q_rope, cos[:, :, None, :], sin[:, :, None, :])
    k_rope = apply_rope(k_rope, cos, sin)
    scale = MLA_QK_DIM ** -0.5
    n_blk = S // Q_BLOCK
    qn = jnp.moveaxis(q_nope.reshape(B, n_blk, Q_BLOCK, MLA_HEADS, MLA_NOPE_DIM), 1, 0)
    qr = jnp.moveaxis(q_rope.reshape(B, n_blk, Q_BLOCK, MLA_HEADS, MLA_ROPE_DIM), 1, 0)
    key_idx = jnp.arange(S)

    def one_block(args):
        blk, qn_b, qr_b = args
        s = (jnp.einsum('bqhd,bkhd->bhqk', qn_b, k_nope, preferred_element_type=jnp.float32)
             + jnp.einsum('bqhr,bkr->bhqk', qr_b, k_rope, preferred_element_type=jnp.float32)) * scale
        q_idx = blk * Q_BLOCK + jnp.arange(Q_BLOCK)
        mask = key_idx[None, :] <= q_idx[:, None]
        s = jnp.where(mask, s, jnp.finfo(jnp.float32).min)
        p = jax.nn.softmax(s, axis=-1).astype(v.dtype)
        return jnp.einsum('bhqk,bkhd->bqhd', p, v)

    o = lax.map(one_block, (jnp.arange(n_blk), qn, qr))
    o = jnp.moveaxis(o, 0, 1).reshape(B, S, MLA_HEADS * MLA_V_DIM)
    return o @ w_o


def chunk_gated_delta_rule(q, k, v, g, beta):
    B, S, H, DK = q.shape
    DV = v.shape[-1]
    C = GDN_CHUNK
    N = S // C

    def to_chunks(t):
        t = t.reshape((B, N, C) + t.shape[2:])
        return jnp.moveaxis(t, 3, 2)

    q, k, v, g, beta = (to_chunks(t) for t in (q, k, v, g, beta))
    g_cum = jnp.cumsum(g, axis=-1)
    tril = jnp.tril(jnp.ones((C, C), dtype=bool))
    tril_strict = jnp.tril(jnp.ones((C, C), dtype=bool), -1)
    diff = g_cum[..., :, None] - g_cum[..., None, :]
    decay = jnp.exp(jnp.where(tril, diff, -jnp.inf))
    k_beta = k * beta[..., None]
    v_beta = v * beta[..., None]
    L = jnp.where(tril_strict, jnp.einsum('bnhcd,bnhed->bnhce', k_beta, k) * decay, 0.0)
    eye = jnp.eye(C, dtype=jnp.float32)
    T = lax.linalg.triangular_solve(eye + L, jnp.broadcast_to(eye, L.shape),
                                    left_side=True, lower=True)
    u = T @ v_beta
    w = T @ (k_beta * jnp.exp(g_cum)[..., None])
    attn_intra = jnp.einsum('bnhcd,bnhed->bnhce', q, k) * decay
    q_dec = q * jnp.exp(g_cum)[..., None]
    k_dec = k * jnp.exp(g_cum[..., -1:] - g_cum)[..., None]
    g_tot = jnp.exp(g_cum[..., -1])

    def step(state, inp):
        w_c, u_c, q_c, k_c, a_c, gt = inp
        v_new = u_c - jnp.einsum('bhcd,bhdv->bhcv', w_c, state)
        o_c = jnp.einsum('bhcd,bhdv->bhcv', q_c, state) + jnp.einsum('bhce,bhev->bhcv', a_c, v_new)
        state = state * gt[..., None, None] + jnp.einsum('bhcd,bhcv->bhdv', k_c, v_new)
        return state, o_c

    xs = tuple(jnp.moveaxis(t, 1, 0) for t in (w, u, q_dec, k_dec, attn_intra, g_tot))
    state0 = jnp.zeros((B, H, DK, DV), jnp.float32)
    _, o = lax.scan(step, state0, xs)
    return o.transpose(1, 0, 3, 2, 4).reshape(B, S, H, DV)


def gdn_mixer(h, w_in, conv_w, a_log, dt_bias, out_norm, w_o):
    B, S, _ = h.shape
    proj = h @ w_in
    qkv, z, b, a = jnp.split(proj, [GDN_CONV_DIM, GDN_CONV_DIM + GDN_VALUE_DIM,
                                    GDN_CONV_DIM + GDN_VALUE_DIM + GDN_V_HEADS], axis=-1)
    qkv = jax.nn.silu(causal_dwconv(qkv, conv_w)).astype(jnp.float32)
    q, k, v = jnp.split(qkv, [GDN_KEY_DIM, 2 * GDN_KEY_DIM], axis=-1)
    q = l2norm(q.reshape(B, S, GDN_QK_HEADS, GDN_HEAD_DIM_K)) * (GDN_HEAD_DIM_K ** -0.5)
    k = l2norm(k.reshape(B, S, GDN_QK_HEADS, GDN_HEAD_DIM_K))
    v = v.reshape(B, S, GDN_V_HEADS, GDN_HEAD_DIM_V)
    rep = GDN_V_HEADS // GDN_QK_HEADS
    q = jnp.repeat(q, rep, axis=2)
    k = jnp.repeat(k, rep, axis=2)
    beta = jax.nn.sigmoid(b.astype(jnp.float32))
    g = -jnp.exp(a_log.astype(jnp.float32)) * jax.nn.softplus(
        a.astype(jnp.float32) + dt_bias.astype(jnp.float32))
    o = chunk_gated_delta_rule(q, k, v, g, beta)
    zf = z.astype(jnp.float32).reshape(B, S, GDN_V_HEADS, GDN_HEAD_DIM_V)
    o = rmsnorm(o, out_norm) * jax.nn.silu(zf)
    return o.reshape(B, S, GDN_VALUE_DIM).astype(h.dtype) @ w_o


def conv_ffn(h, w_up, conv_w, conv_b, w_down):
    u = causal_dwconv(h @ w_up, conv_w) + conv_b
    gate, up = jnp.split(u, 2, axis=-1)
    return (jax.nn.silu(gate) * up) @ w_down


def setup_inputs(seed: int = 0) -> dict:
    key = jax.random.key(seed)
    ks = iter(jax.random.split(key, 32))

    def nrm(shape, scale):
        return jax.random.normal(next(ks), shape, jnp.float32) * scale

    def gain(shape):
        return 1.0 + nrm(shape, 0.02)

    D = D_MODEL
    x = jax.random.normal(next(ks), (BATCH, SEQ, D), jnp.float32)
    offs = jax.random.randint(next(ks), (BATCH, 1), 0, 1024, dtype=jnp.int32)
    positions = (jnp.arange(SEQ, dtype=jnp.int32)[None, :] + offs).astype(jnp.int32)
    nm, ng = N_MLA_LAYERS, N_GDN_LAYERS
    A = jax.random.uniform(next(ks), (ng, GDN_V_HEADS), jnp.float32, 1.0, 16.0)
    dt = jnp.exp(jax.random.uniform(next(ks), (ng, GDN_V_HEADS), jnp.float32,
                                    math.log(1e-3), math.log(0.1)))
    return {
        "x": x,
        "positions": positions,
        "mla_norm": gain((nm, D)),
        "mla_w_in": nrm((nm, D, MLA_IN_DIM), D ** -0.5),
        "mla_q_norm": gain((nm, MLA_Q_RANK)),
        "mla_kv_norm": gain((nm, MLA_KV_RANK)),
        "mla_w_uq": nrm((nm, MLA_Q_RANK, MLA_HEADS * MLA_QK_DIM), MLA_Q_RANK ** -0.5),
        "mla_w_ukv": nrm((nm, MLA_KV_RANK, MLA_HEADS * (MLA_NOPE_DIM + MLA_V_DIM)), MLA_KV_RANK ** -0.5),
        "mla_w_o": nrm((nm, MLA_HEADS * MLA_V_DIM, D), (MLA_HEADS * MLA_V_DIM) ** -0.5),
        "gdn_norm": gain((ng, D)),
        "gdn_w_in": nrm((ng, D, GDN_IN_DIM), D ** -0.5),
        "gdn_conv_w": nrm((ng, GDN_CONV, GDN_CONV_DIM), GDN_CONV ** -0.5),
        "gdn_a_log": jnp.log(A),
        "gdn_dt_bias": dt + jnp.log(-jnp.expm1(-dt)),
        "gdn_out_norm": gain((ng, GDN_HEAD_DIM_V)),
        "gdn_w_o": nrm((ng, GDN_VALUE_DIM, D), GDN_VALUE_DIM ** -0.5),
        "ffn_norm": gain((DEPTH, D)),
        "ffn_w_up": nrm((DEPTH, D, 2 * D_FF), D ** -0.5),
        "ffn_conv_w": nrm((DEPTH, FFN_CONV, 2 * D_FF), FFN_CONV ** -0.5),
        "ffn_conv_b": nrm((DEPTH, 2 * D_FF), 0.01),
        "ffn_w_down": nrm((DEPTH, D_FF, D), D_FF ** -0.5),
        "final_norm": gain((D,)),
    }


def reference(x, positions, mla_norm, mla_w_in, mla_q_norm, mla_kv_norm, mla_w_uq, mla_w_ukv,
              mla_w_o, gdn_norm, gdn_w_in, gdn_conv_w, gdn_a_log, gdn_dt_bias, gdn_out_norm,
              gdn_w_o, ffn_norm, ffn_w_up, ffn_conv_w, ffn_conv_b, ffn_w_down, final_norm):
    h = x
    for i in range(DEPTH):
        j = i // N_MIXERS
        if i % N_MIXERS == 0:
            h = h + mla_mixer(rmsnorm(h, mla_norm[j]), positions, mla_w_in[j], mla_q_norm[j],
                              mla_kv_norm[j], mla_w_uq[j], mla_w_ukv[j], mla_w_o[j])
        else:
            h = h + gdn_mixer(rmsnorm(h, gdn_norm[j]), gdn_w_in[j], gdn_conv_w[j], gdn_a_log[j],
                              gdn_dt_bias[j], gdn_out_norm[j], gdn_w_o[j])
        h = h + conv_ffn(rmsnorm(h, ffn_norm[i]), ffn_w_up[i], ffn_conv_w[i], ffn_conv_b[i],
                         ffn_w_down[i])
    return rmsnorm(h, final_norm)
```

```python
import functools
import math

import jax
import jax.numpy as jnp
from jax import lax
from jax.experimental import pallas as pl
from jax.experimental.pallas import tpu as pltpu

F32 = jnp.float32
BF16 = jnp.bfloat16

RMS_EPS = 1e-6
L2_EPS = 1e-6

MLA_HEADS = 16
MLA_Q_RANK = 512
MLA_KV_RANK = 512
MLA_NOPE_DIM = 128
MLA_ROPE_DIM = 64
MLA_V_DIM = 128
MLA_QK_DIM = MLA_NOPE_DIM + MLA_ROPE_DIM
ROPE_THETA = 10000.0

GDN_QK_HEADS = 16
GDN_V_HEADS = 32
GDN_HEAD_DIM = 128
GDN_KEY_DIM = GDN_QK_HEADS * GDN_HEAD_DIM
GDN_VALUE_DIM = GDN_V_HEADS * GDN_HEAD_DIM
GDN_CONV_DIM = 2 * GDN_KEY_DIM + GDN_VALUE_DIM
GDN_CHUNK = 64

VMEM_LIMIT_BYTES = 56 * 1024 * 1024
SUBLANES = 8
NEG_BIG = -1e30


def _params(*sem):
    return pltpu.CompilerParams(dimension_semantics=sem, vmem_limit_bytes=VMEM_LIMIT_BYTES)


def _rms(x, w):
    return x * lax.rsqrt(jnp.mean(x * x, axis=-1, keepdims=True) + RMS_EPS) * w


def _sigmoid(x):
    return 1.0 / (1.0 + jnp.exp(-x))


def _rmsnorm_body(x_ref, w_ref, o_ref):
    o_ref[...] = _rms(x_ref[...], w_ref[...]).astype(o_ref.dtype)


def rmsnorm(x, w, out_dtype, tm=512):
    t, d = x.shape
    return pl.pallas_call(
        _rmsnorm_body,
        grid=(t // tm,),
        in_specs=[pl.BlockSpec((tm, d), lambda i: (i, 0)),
                  pl.BlockSpec((1, d), lambda i: (0, 0))],
        out_specs=pl.BlockSpec((tm, d), lambda i: (i, 0)),
        out_shape=jax.ShapeDtypeStruct((t, d), out_dtype),
        compiler_params=_params("parallel"),
        name="rmsnorm",
    )(x, w.reshape(1, d))


def _matmul_body(a_ref, b_ref, o_ref):
    o_ref[...] = jnp.dot(a_ref[...], b_ref[...], preferred_element_type=F32).astype(o_ref.dtype)


def _matmul_res_body(a_ref, b_ref, r_ref, o_ref):
    acc = jnp.dot(a_ref[...], b_ref[...], preferred_element_type=F32)
    o_ref[...] = (r_ref[...] + acc).astype(o_ref.dtype)


def matmul(a, b, *, n, b_col0=0, res=None, out_dtype=F32, tm, tn, name):
    m, k = a.shape
    joff = b_col0 // tn
    in_specs = [pl.BlockSpec((tm, k), lambda i, j: (i, 0)),
                pl.BlockSpec((k, tn), lambda i, j: (0, j + joff))]
    args = [a, b]
    body = _matmul_body
    if res is not None:
        in_specs.append(pl.BlockSpec((tm, tn), lambda i, j: (i, j)))
        args.append(res)
        body = _matmul_res_body
    return pl.pallas_call(
        body,
        grid=(m // tm, n // tn),
        in_specs=in_specs,
        out_specs=pl.BlockSpec((tm, tn), lambda i, j: (i, j)),
        out_shape=jax.ShapeDtypeStruct((m, n), out_dtype),
        compiler_params=_params("parallel", "parallel"),
        name=name,
    )(*args)


def _causal_conv(u, buf_ref, carry_ref, j, first, cw_ref, width):
    tm = u.shape[0]

    @pl.when(first)
    def _():
        buf_ref[0:SUBLANES, :] = jnp.zeros((SUBLANES, u.shape[1]), F32)

    @pl.when(jnp.logical_not(first))
    def _():
        buf_ref[0:SUBLANES, :] = carry_ref[j]

    buf_ref[SUBLANES:SUBLANES + tm, :] = u
    carry_ref[j] = u[tm - SUBLANES:tm, :]
    y = cw_ref[width - 1:width, :] * u
    for d in range(1, width):
        y = y + cw_ref[width - 1 - d:width - d, :] * buf_ref[SUBLANES - d:SUBLANES - d + tm, :]
    return y


def _ffn_up_body(tiles_per_seq, x_ref, wg_ref, wu_ref, cwg_ref, cwu_ref, bg_ref, bu_ref, o_ref,
                 gbuf, ubuf, gcarry, ucarry):
    i = pl.program_id(0)
    j = pl.program_id(1)
    first = (i % tiles_per_seq) == 0
    x = x_ref[...]
    ug = jnp.dot(x, wg_ref[...], preferred_element_type=F32)
    uu = jnp.dot(x, wu_ref[...], preferred_element_type=F32)
    yg = _causal_conv(ug, gbuf, gcarry, j, first, cwg_ref, 3) + bg_ref[...]
    yu = _causal_conv(uu, ubuf, ucarry, j, first, cwu_ref, 3) + bu_ref[...]
    o_ref[...] = (yg * _sigmoid(yg) * yu).astype(o_ref.dtype)


def ffn_up(hn, w_up, conv_w, conv_b, seq, tm=1024, tf=512):
    t, d = hn.shape
    f = w_up.shape[1] // 2
    nj = f // tf
    return pl.pallas_call(
        functools.partial(_ffn_up_body, seq // tm),
        grid=(t // tm, nj),
        in_specs=[pl.BlockSpec((tm, d), lambda i, j: (i, 0)),
                  pl.BlockSpec((d, tf), lambda i, j: (0, j)),
                  pl.BlockSpec((d, tf), lambda i, j: (0, j + nj)),
                  pl.BlockSpec((3, tf), lambda i, j: (0, j)),
                  pl.BlockSpec((3, tf), lambda i, j: (0, j + nj)),
                  pl.BlockSpec((1, tf), lambda i, j: (0, j)),
                  pl.BlockSpec((1, tf), lambda i, j: (0, j + nj))],
        out_specs=pl.BlockSpec((tm, tf), lambda i, j: (i, j)),
        out_shape=jax.ShapeDtypeStruct((t, f), BF16),
        scratch_shapes=[pltpu.VMEM((tm + SUBLANES, tf), F32),
                        pltpu.VMEM((tm + SUBLANES, tf), F32),
                        pltpu.VMEM((nj, SUBLANES, tf), F32),
                        pltpu.VMEM((nj, SUBLANES, tf), F32)],
        compiler_params=_params("arbitrary", "arbitrary"),
        name="ffn_up",
    )(hn, w_up, w_up, conv_w, conv_w, conv_b.reshape(1, -1), conv_b.reshape(1, -1))


def _gdn_qkv_body(tiles_per_seq, n_q_tiles, n_qk_tiles, x_ref, w_ref, cw_ref, o_ref, buf, carry):
    i = pl.program_id(0)
    j = pl.program_id(1)
    first = (i % tiles_per_seq) == 0
    u = jnp.dot(x_ref[...], w_ref[...], preferred_element_type=F32)
    y = _causal_conv(u, buf, carry, j, first, cw_ref, 4)
    y = y * _sigmoid(y)

    @pl.when(j >= n_qk_tiles)
    def _():
        o_ref[...] = y.astype(o_ref.dtype)

    @pl.when(j < n_qk_tiles)
    def _():
        scale = jnp.where(j < n_q_tiles, GDN_HEAD_DIM ** -0.5, 1.0).astype(F32)
        for g in range(y.shape[1] // GDN_HEAD_DIM):
            blk = y[:, g * GDN_HEAD_DIM:(g + 1) * GDN_HEAD_DIM]
            nrm = lax.rsqrt(jnp.sum(blk * blk, axis=-1, keepdims=True) + L2_EPS) * scale
            o_ref[:, g * GDN_HEAD_DIM:(g + 1) * GDN_HEAD_DIM] = (blk * nrm).astype(o_ref.dtype)


def gdn_qkv(hn, w_in, conv_w, seq, tm=1024, tn=512):
    t, d = hn.shape
    nj = GDN_CONV_DIM // tn
    return pl.pallas_call(
        functools.partial(_gdn_qkv_body, seq // tm, GDN_KEY_DIM // tn, 2 * GDN_KEY_DIM // tn),
        grid=(t // tm, nj),
        in_specs=[pl.BlockSpec((tm, d), lambda i, j: (i, 0)),
                  pl.BlockSpec((d, tn), lambda i, j: (0, j)),
                  pl.BlockSpec((4, tn), lambda i, j: (0, j))],
        out_specs=pl.BlockSpec((tm, tn), lambda i, j: (i, j)),
        out_shape=jax.ShapeDtypeStruct((t, GDN_CONV_DIM), BF16),
        scratch_shapes=[pltpu.VMEM((tm + SUBLANES, tn), F32),
                        pltpu.VMEM((nj, SUBLANES, tn), F32)],
        compiler_params=_params("arbitrary", "arbitrary"),
        name="gdn_qkv",
    )(hn, w_in, conv_w)


def _gdn_gate_body(x_ref, w_ref, alog_ref, dtb_ref, beta_ref, gc_ref):
    tm = x_ref.shape[0]
    nh = GDN_V_HEADS
    ba = jnp.dot(x_ref[...], w_ref[...], preferred_element_type=F32)
    b = ba[:, 0:nh]
    a = ba[:, nh:2 * nh] + dtb_ref[...]
    beta_ref[...] = _sigmoid(b)
    softplus = jnp.maximum(a, 0.0) + jnp.log(1.0 + jnp.exp(-jnp.abs(a)))
    g = -jnp.exp(alog_ref[...]) * softplus
    row = lax.broadcasted_iota(jnp.int32, (tm, tm), 0)
    col = lax.broadcasted_iota(jnp.int32, (tm, tm), 1)
    same_chunk = (row // GDN_CHUNK) == (col // GDN_CHUNK)
    tri = jnp.where(jnp.logical_and(same_chunk, col <= row), 1.0, 0.0).astype(F32)
    gc_ref[...] = jnp.dot(tri, g, preferred_element_type=F32, precision=lax.Precision.HIGHEST)


def gdn_gates(hn, w_ba, a_log, dt_bias, tm=256):
    t, d = hn.shape
    nh = GDN_V_HEADS
    return pl.pallas_call(
        _gdn_gate_body,
        grid=(t // tm,),
        in_specs=[pl.BlockSpec((tm, d), lambda i: (i, 0)),
                  pl.BlockSpec((d, 128), lambda i: (0, 0)),
                  pl.BlockSpec((1, nh), lambda i: (0, 0)),
                  pl.BlockSpec((1, nh), lambda i: (0, 0))],
        out_specs=[pl.BlockSpec((tm, nh), lambda i: (i, 0)),
                   pl.BlockSpec((tm, nh), lambda i: (i, 0))],
        out_shape=[jax.ShapeDtypeStruct((t, nh), F32),
                   jax.ShapeDtypeStruct((t, nh), F32)],
        compiler_params=_params("parallel"),
        name="gdn_gates",
    )(hn, w_ba, a_log.reshape(1, nh), dt_bias.reshape(1, nh))


def _dot_t(a, b):
    return lax.dot_general(a, b, (((1,), (1,)), ((), ())), preferred_element_type=F32)


def _dot_hi(a, b):
    return jnp.dot(a, b, preferred_element_type=F32, precision=lax.Precision.HIGHEST)


def _gdn_chunk_body(q_ref, k_ref, v_ref, z_ref, gc_ref, gct_ref, beta_ref, nw_ref, o_ref, s_ref):
    c = GDN_CHUNK
    dh = GDN_HEAD_DIM
    tt = q_ref.shape[0]

    @pl.when(pl.program_id(2) == 0)
    def _():
        s_ref[...] = jnp.zeros(s_ref.shape, F32)

    row = lax.broadcasted_iota(jnp.int32, (c, c), 0)
    col = lax.broadcasted_iota(jnp.int32, (c, c), 1)
    tril = col <= row
    strict = col < row
    eye = jnp.where(col == row, 1.0, 0.0).astype(F32)

    for ci in range(tt // c):
        r = slice(ci * c, (ci + 1) * c)
        q = q_ref[r, :]
        k = k_ref[r, :]
        kf = k.astype(F32)
        qf = q.astype(F32)
        kk = _dot_t(k, k)
        qk = _dot_t(q, k)
        for j in range(2):
            hs = slice(j * dh, (j + 1) * dh)
            gcol = gc_ref[r, j:j + 1]
            grow = gct_ref[j:j + 1, r]
            bcol = beta_ref[r, j:j + 1]
            dec = jnp.exp(jnp.where(tril, gcol - grow, NEG_BIG))
            a = jnp.where(strict, -(bcol * kk) * dec, 0.0)
            tmat = eye + a
            apow = a
            for _ in range(5):
                apow = _dot_hi(apow, apow)
                tmat = tmat + _dot_hi(tmat, apow)
            tb = tmat.astype(BF16)
            egc = jnp.exp(gcol)
            vb = (v_ref[r, hs].astype(F32) * bcol).astype(BF16)
            kb = (kf * (bcol * egc)).astype(BF16)
            u = jnp.dot(tb, vb, preferred_element_type=F32)
            w = jnp.dot(tb, kb, preferred_element_type=F32)
            attn = (qk * dec).astype(BF16)
            glast = gcol[c - 1:c, :]
            q_dec = (qf * egc).astype(BF16)
            k_dec = (kf * jnp.exp(glast - gcol)).astype(BF16)
            state = s_ref[j]
            sb = state.astype(BF16)
            v_new = u - jnp.dot(w.astype(BF16), sb, preferred_element_type=F32)
            vnb = v_new.astype(BF16)
            o = (jnp.dot(q_dec, sb, preferred_element_type=F32)
                 + jnp.dot(attn, vnb, preferred_element_type=F32))
            s_ref[j] = state * jnp.exp(glast) + lax.dot_general(
                k_dec, vnb, (((0,), (0,)), ((), ())), preferred_element_type=F32)
            zf = z_ref[r, hs].astype(F32)
            o_ref[r, hs] = (_rms(o, nw_ref[...]) * (zf * _sigmoid(zf))).astype(o_ref.dtype)


def gdn_chunks(qkv, z, gc, gct, beta, out_norm, batch, seq, tt=256):
    t = qkv.shape[0]
    nt = seq // tt
    hk = GDN_QK_HEADS
    return pl.pallas_call(
        _gdn_chunk_body,
        grid=(batch, hk, nt),
        in_specs=[pl.BlockSpec((tt, 128), lambda b, h, n: (b * nt + n, h)),
                  pl.BlockSpec((tt, 128), lambda b, h, n: (b * nt + n, hk + h)),
                  pl.BlockSpec((tt, 256), lambda b, h, n: (b * nt + n, hk + h)),
                  pl.BlockSpec((tt, 256), lambda b, h, n: (b * nt + n, h)),
                  pl.BlockSpec((None, tt, 2), lambda b, h, n: (h, b * nt + n, 0)),
                  pl.BlockSpec((None, 2, tt), lambda b, h, n: (h, 0, b * nt + n)),
                  pl.BlockSpec((None, tt, 2), lambda b, h, n: (h, b * nt + n, 0)),
                  pl.BlockSpec((1, 128), lambda b, h, n: (0, 0))],
        out_specs=pl.BlockSpec((tt, 256), lambda b, h, n: (b * nt + n, h)),
        out_shape=jax.ShapeDtypeStruct((t, GDN_VALUE_DIM), BF16),
        scratch_shapes=[pltpu.VMEM((2, 128, 128), F32)],
        compiler_params=_params("parallel", "parallel", "arbitrary"),
        name="gdn_chunks",
    )(qkv, qkv, qkv, z, gc, gct, beta, out_norm.reshape(1, 128))


def _rope_table_body(pos_ref, freq_ref, sign_ref, cc_ref, ss_ref):
    ang = pos_ref[...].astype(F32) * freq_ref[...]
    cc_ref[...] = jnp.cos(ang)
    ss_ref[...] = jnp.sin(ang) * sign_ref[...]


def rope_tables(positions, tm=2048):
    t = positions.size
    half = MLA_ROPE_DIM // 2
    inv_freq = ROPE_THETA ** (-jnp.arange(0, MLA_ROPE_DIM, 2, dtype=F32) / MLA_ROPE_DIM)
    freq = jnp.concatenate([inv_freq, inv_freq]).reshape(1, MLA_ROPE_DIM)
    sign = jnp.concatenate([-jnp.ones((half,), F32), jnp.ones((half,), F32)]).reshape(1, MLA_ROPE_DIM)
    tm = min(tm, t)
    return pl.pallas_call(
        _rope_table_body,
        grid=(t // tm,),
        in_specs=[pl.BlockSpec((tm, 1), lambda i: (i, 0)),
                  pl.BlockSpec((1, MLA_ROPE_DIM), lambda i: (0, 0)),
                  pl.BlockSpec((1, MLA_ROPE_DIM), lambda i: (0, 0))],
        out_specs=[pl.BlockSpec((tm, MLA_ROPE_DIM), lambda i: (i, 0)),
                   pl.BlockSpec((tm, MLA_ROPE_DIM), lambda i: (i, 0))],
        out_shape=[jax.ShapeDtypeStruct((t, MLA_ROPE_DIM), F32),
                   jax.ShapeDtypeStruct((t, MLA_ROPE_DIM), F32)],
        compiler_params=_params("parallel"),
        name="rope_tables",
    )(positions.reshape(t, 1), freq, sign)


def _mla_proj_body(x_ref, nw_ref, win_ref, qnw_ref, kvnw_ref, wq_ref, wkv_ref, cc_ref, ss_ref,
                   q_ref, k_ref, v_ref):
    nope, rope = MLA_NOPE_DIM, MLA_ROPE_DIM
    hw = nope + 2 * rope
    xn = _rms(x_ref[...], nw_ref[...]).astype(BF16)
    c = jnp.dot(xn, win_ref[...], preferred_element_type=F32)
    cq = _rms(c[:, 0:MLA_Q_RANK], qnw_ref[...]).astype(BF16)
    ckv = _rms(c[:, MLA_Q_RANK:MLA_Q_RANK + MLA_KV_RANK], kvnw_ref[...]).astype(BF16)
    cc = cc_ref[...]
    ss = ss_ref[...]
    r0 = MLA_Q_RANK + MLA_KV_RANK
    kr = (c[:, r0:r0 + rope] * cc + c[:, r0 + rope:r0 + 2 * rope] * ss).astype(BF16)
    scale = MLA_QK_DIM ** -0.5
    for h in range(MLA_HEADS):
        aq = jnp.dot(cq, wq_ref[:, h * hw:(h + 1) * hw], preferred_element_type=F32)
        q_ref[h, :, 0:nope] = (aq[:, 0:nope] * scale).astype(BF16)
        qr = aq[:, nope:nope + rope] * cc + aq[:, nope + rope:nope + 2 * rope] * ss
        q_ref[h, :, nope:nope + rope] = (qr * scale).astype(BF16)
        akv = jnp.dot(ckv, wkv_ref[:, h * 256:(h + 1) * 256], preferred_element_type=F32)
        k_ref[h, :, 0:nope] = akv[:, 0:nope].astype(BF16)
        k_ref[h, :, nope:nope + rope] = kr
        v_ref[h] = akv[:, nope:nope + MLA_V_DIM].astype(BF16)


def _swap_halves(w):
    half = w.shape[-1] // 2
    return jnp.concatenate([w[..., half:], w[..., :half]], axis=-1)


def mla_proj(x, norm_w, w_in, q_norm, kv_norm, w_uq, w_ukv, cc, ss, tm=256):
    t, d = x.shape
    nh, nope, rope = MLA_HEADS, MLA_NOPE_DIM, MLA_ROPE_DIM
    r0 = MLA_Q_RANK + MLA_KV_RANK
    w_kr = w_in[:, r0:r0 + rope]
    w_in_ext = jnp.concatenate([w_in[:, :r0], w_kr, _swap_halves(w_kr)], axis=1).astype(BF16)
    wq = w_uq.reshape(MLA_Q_RANK, nh, MLA_QK_DIM)
    wq_ext = jnp.concatenate([wq, _swap_halves(wq[..., nope:])], axis=-1)
    hw = nope + 2 * rope
    wq_ext = wq_ext.reshape(MLA_Q_RANK, nh * hw).astype(BF16)
    wkv = w_ukv.astype(BF16)
    const = lambda i: (0, 0)
    return pl.pallas_call(
        _mla_proj_body,
        grid=(t // tm,),
        in_specs=[pl.BlockSpec((tm, d), lambda i: (i, 0)),
                  pl.BlockSpec((1, d), const),
                  pl.BlockSpec(w_in_ext.shape, const),
                  pl.BlockSpec((1, MLA_Q_RANK), const),
                  pl.BlockSpec((1, MLA_KV_RANK), const),
                  pl.BlockSpec(wq_ext.shape, const),
                  pl.BlockSpec(wkv.shape, const),
                  pl.BlockSpec((tm, rope), lambda i: (i, 0)),
                  pl.BlockSpec((tm, rope), lambda i: (i, 0))],
        out_specs=[pl.BlockSpec((nh, tm, MLA_QK_DIM), lambda i: (0, i, 0)),
                   pl.BlockSpec((nh, tm, MLA_QK_DIM), lambda i: (0, i, 0)),
                   pl.BlockSpec((nh, tm, MLA_V_DIM), lambda i: (0, i, 0))],
        out_shape=[jax.ShapeDtypeStruct((nh, t, MLA_QK_DIM), BF16),
                   jax.ShapeDtypeStruct((nh, t, MLA_QK_DIM), BF16),
                   jax.ShapeDtypeStruct((nh, t, MLA_V_DIM), BF16)],
        compiler_params=_params("parallel"),
        name="mla_proj",
    )(x, norm_w.reshape(1, d), w_in_ext, q_norm.reshape(1, -1), kv_norm.reshape(1, -1),
      wq_ext, wkv, cc, ss)


def _attn_body(q_ref, k_ref, v_ref, o_ref, m_ref, l_ref, acc_ref):
    tq = q_ref.shape[0]
    qi = pl.program_id(2)
    q = q_ref[...]
    m_ref[...] = jnp.full(m_ref.shape, NEG_BIG, F32)
    l_ref[...] = jnp.zeros(l_ref.shape, F32)
    acc_ref[...] = jnp.zeros(acc_ref.shape, F32)

    def step(kb, masked):
        start = pl.multiple_of(kb * tq, tq)
        k = k_ref[pl.ds(start, tq), :]
        v = v_ref[pl.ds(start, tq), :]
        s = _dot_t(q, k)
        if masked:
            row = lax.broadcasted_iota(jnp.int32, (tq, tq), 0)
            col = lax.broadcasted_iota(jnp.int32, (tq, tq), 1)
            s = jnp.where(col <= row, s, NEG_BIG)
        m_prev = m_ref[...]
        m_new = jnp.maximum(m_prev, jnp.max(s, axis=-1, keepdims=True))
        alpha = jnp.exp(m_prev - m_new)
        p = jnp.exp(s - m_new)
        l_ref[...] = alpha * l_ref[...] + jnp.sum(p, axis=-1, keepdims=True)
        acc_ref[...] = alpha * acc_ref[...] + jnp.dot(p.astype(BF16), v, preferred_element_type=F32)
        m_ref[...] = m_new

    def full_step(kb, carry):
        step(kb, False)
        return carry

    lax.fori_loop(0, qi, full_step, 0)
    step(qi, True)
    o_ref[...] = (acc_ref[...] / l_ref[...]).astype(o_ref.dtype)


def attention(q, k, v, batch, seq, tq=512):
    nh, t, _ = q.shape
    nq = seq // tq
    return pl.pallas_call(
        _attn_body,
        grid=(batch, nh, nq),
        in_specs=[pl.BlockSpec((None, tq, MLA_QK_DIM), lambda b, h, i: (h, b * nq + i, 0)),
                  pl.BlockSpec((None, seq, MLA_QK_DIM), lambda b, h, i: (h, b, 0)),
                  pl.BlockSpec((None, seq, MLA_V_DIM), lambda b, h, i: (h, b, 0))],
        out_specs=pl.BlockSpec((tq, MLA_V_DIM), lambda b, h, i: (b * nq + i, h)),
        out_shape=jax.ShapeDtypeStruct((t, nh * MLA_V_DIM), BF16),
        scratch_shapes=[pltpu.VMEM((tq, 1), F32),
                        pltpu.VMEM((tq, 1), F32),
                        pltpu.VMEM((tq, MLA_V_DIM), F32)],
        compiler_params=_params("parallel", "parallel", "arbitrary"),
        name="mla_attention",
    )(q, k, v)


def _ffn_layer(h, norm_w, w_up, conv_w, conv_b, w_down, seq):
    hn = rmsnorm(h, norm_w, BF16)
    act = ffn_up(hn, w_up.astype(BF16), conv_w, conv_b, seq)
    return matmul(act, w_down.astype(BF16), n=w_down.shape[1], res=h, tm=1024, tn=512,
                  name="ffn_down")


def _mla_layer(h, positions, norm_w, w_in, q_norm, kv_norm, w_uq, w_ukv, w_o, batch, seq):
    cc, ss = rope_tables(positions)
    q, k, v = mla_proj(h, norm_w, w_in, q_norm, kv_norm, w_uq, w_ukv, cc, ss)
    o = attention(q, k, v, batch, seq)
    return matmul(o, w_o.astype(BF16), n=w_o.shape[1], res=h, tm=1024, tn=1024, name="mla_out")


def _gdn_layer(h, norm_w, w_in, conv_w, a_log, dt_bias, out_norm, w_o, batch, seq):
    t = h.shape[0]
    hn = rmsnorm(h, norm_w, BF16)
    w_in_b = w_in.astype(BF16)
    qkv = gdn_qkv(hn, w_in_b, conv_w, seq)
    z = matmul(hn, w_in_b, n=GDN_VALUE_DIM, b_col0=GDN_CONV_DIM, out_dtype=BF16, tm=1024, tn=512,
               name="gdn_z")
    w_ba = jnp.pad(w_in_b[:, GDN_CONV_DIM + GDN_VALUE_DIM:], ((0, 0), (0, 128 - 2 * GDN_V_HEADS)))
    beta, gc = gdn_gates(hn, w_ba, a_log, dt_bias)
    hk = GDN_QK_HEADS
    gc3 = gc.reshape(t, hk, 2)
    gc_col = gc3.transpose(1, 0, 2)
    gc_row = gc3.transpose(1, 2, 0)
    beta_col = beta.reshape(t, hk, 2).transpose(1, 0, 2)
    o = gdn_chunks(qkv, z, gc_col, gc_row, beta_col, out_norm, batch, seq)
    return matmul(o, w_o.astype(BF16), n=w_o.shape[1], res=h, tm=1024, tn=1024, name="gdn_out")


def kernel(x, positions, mla_norm, mla_w_in, mla_q_norm, mla_kv_norm, mla_w_uq, mla_w_ukv, mla_w_o,
           gdn_norm, gdn_w_in, gdn_conv_w, gdn_a_log, gdn_dt_bias, gdn_out_norm, gdn_w_o, ffn_norm,
           ffn_w_up, ffn_conv_w, ffn_conv_b, ffn_w_down, final_norm):
    batch, seq, d = x.shape
    h = x.reshape(batch * seq, d)
    depth = ffn_norm.shape[0]
    for i in range(depth):
        j = i // 2
        if i % 2 == 0:
            h = _mla_layer(h, positions, mla_norm[j], mla_w_in[j], mla_q_norm[j], mla_kv_norm[j],
                           mla_w_uq[j], mla_w_ukv[j], mla_w_o[j], batch, seq)
        else:
            h = _gdn_layer(h, gdn_norm[j], gdn_w_in[j], gdn_conv_w[j], gdn_a_log[j], gdn_dt_bias[j],
                           gdn_out_norm[j], gdn_w_o[j], batch, seq)
        h = _ffn_layer(h, ffn_norm[i], ffn_w_up[i], ffn_conv_w[i], ffn_conv_b[i], ffn_w_down[i], seq)
    return rmsnorm(h, final_norm, x.dtype).reshape(batch, seq, d)
```

```python
import functools
import math

import jax
import jax.numpy as jnp
from jax import lax
from jax.experimental import pallas as pl
from jax.experimental.pallas import tpu as pltpu

F32 = jnp.float32
BF16 = jnp.bfloat16

RMS_EPS = 1e-6
L2_EPS = 1e-6

MLA_HEADS = 16
MLA_Q_RANK = 512
MLA_KV_RANK = 512
MLA_NOPE_DIM = 128
MLA_ROPE_DIM = 64
MLA_V_DIM = 128
MLA_QK_DIM = MLA_NOPE_DIM + MLA_ROPE_DIM
ROPE_THETA = 10000.0

GDN_QK_HEADS = 16
GDN_V_HEADS = 32
GDN_HEAD_DIM = 128
GDN_KEY_DIM = GDN_QK_HEADS * GDN_HEAD_DIM
GDN_VALUE_DIM = GDN_V_HEADS * GDN_HEAD_DIM
GDN_CONV_DIM = 2 * GDN_KEY_DIM + GDN_VALUE_DIM
GDN_CHUNK = 64

VMEM_LIMIT_BYTES = 56 * 1024 * 1024
SUBLANES = 8
NEG_BIG = -1e30


def _params(*sem):
    return pltpu.CompilerParams(dimension_semantics=sem, vmem_limit_bytes=VMEM_LIMIT_BYTES)


def _rms(x, w):
    return x * lax.rsqrt(jnp.mean(x * x, axis=-1, keepdims=True) + RMS_EPS) * w


def _sigmoid(x):
    return 1.0 / (1.0 + jnp.exp(-x))


def _rmsnorm_body(x_ref, w_ref, o_ref):
    o_ref[...] = _rms(x_ref[...], w_ref[...]).astype(o_ref.dtype)


def rmsnorm(x, w, out_dtype, tm=512):
    t, d = x.shape
    return pl.pallas_call(
        _rmsnorm_body,
        grid=(t // tm,),
        in_specs=[pl.BlockSpec((tm, d), lambda i: (i, 0)),
                  pl.BlockSpec((1, d), lambda i: (0, 0))],
        out_specs=pl.BlockSpec((tm, d), lambda i: (i, 0)),
        out_shape=jax.ShapeDtypeStruct((t, d), out_dtype),
        compiler_params=_params("parallel"),
        name="rmsnorm",
    )(x, w.reshape(1, d))


def _matmul_body(a_ref, b_ref, o_ref):
    o_ref[...] = jnp.dot(a_ref[...], b_ref[...], preferred_element_type=F32).astype(o_ref.dtype)


def _matmul_res_body(a_ref, b_ref, r_ref, o_ref):
    acc = jnp.dot(a_ref[...], b_ref[...], preferred_element_type=F32)
    o_ref[...] = (r_ref[...] + acc).astype(o_ref.dtype)


def matmul(a, b, *, n, b_col0=0, res=None, out_dtype=F32, tm, tn, name):
    m, k = a.shape
    joff = b_col0 // tn
    in_specs = [pl.BlockSpec((tm, k), lambda i, j: (i, 0)),
                pl.BlockSpec((k, tn), lambda i, j: (0, j + joff))]
    args = [a, b]
    body = _matmul_body
    if res is not None:
        in_specs.append(pl.BlockSpec((tm, tn), lambda i, j: (i, j)))
        args.append(res)
        body = _matmul_res_body
    return pl.pallas_call(
        body,
        grid=(m // tm, n // tn),
        in_specs=in_specs,
        out_specs=pl.BlockSpec((tm, tn), lambda i, j: (i, j)),
        out_shape=jax.ShapeDtypeStruct((m, n), out_dtype),
        compiler_params=_params("parallel", "parallel"),
        name=name,
    )(*args)


def _causal_conv(u, buf_ref, carry_ref, j, first, cw_ref, width):
    tm = u.shape[0]

    @pl.when(first)
    def _():
        buf_ref[0:SUBLANES, :] = jnp.zeros((SUBLANES, u.shape[1]), F32)

    @pl.when(jnp.logical_not(first))
    def _():
        buf_ref[0:SUBLANES, :] = carry_ref[j]

    buf_ref[SUBLANES:SUBLANES + tm, :] = u
    carry_ref[j] = u[tm - SUBLANES:tm, :]
    y = cw_ref[width - 1:width, :] * u
    for d in range(1, width):
        y = y + cw_ref[width - 1 - d:width - d, :] * buf_ref[SUBLANES - d:SUBLANES - d + tm, :]
    return y


def _ffn_up_body(tiles_per_seq, x_ref, wg_ref, wu_ref, cwg_ref, cwu_ref, bg_ref, bu_ref, o_ref,
                 gbuf, ubuf, gcarry, ucarry):
    i = pl.program_id(0)
    j = pl.program_id(1)
    first = (i % tiles_per_seq) == 0
    x = x_ref[...]
    ug = jnp.dot(x, wg_ref[...], preferred_element_type=F32)
    uu = jnp.dot(x, wu_ref[...], preferred_element_type=F32)
    yg = _causal_conv(ug, gbuf, gcarry, j, first, cwg_ref, 3) + bg_ref[...]
    yu = _causal_conv(uu, ubuf, ucarry, j, first, cwu_ref, 3) + bu_ref[...]
    o_ref[...] = (yg * _sigmoid(yg) * yu).astype(o_ref.dtype)


def ffn_up(hn, w_up, conv_w, conv_b, seq, tm=1024, tf=512):
    t, d = hn.shape
    f = w_up.shape[1] // 2
    nj = f // tf
    return pl.pallas_call(
        functools.partial(_ffn_up_body, seq // tm),
        grid=(t // tm, nj),
        in_specs=[pl.BlockSpec((tm, d), lambda i, j: (i, 0)),
                  pl.BlockSpec((d, tf), lambda i, j: (0, j)),
                  pl.BlockSpec((d, tf), lambda i, j: (0, j + nj)),
                  pl.BlockSpec((3, tf), lambda i, j: (0, j)),
                  pl.BlockSpec((3, tf), lambda i, j: (0, j + nj)),
                  pl.BlockSpec((1, tf), lambda i, j: (0, j)),
                  pl.BlockSpec((1, tf), lambda i, j: (0, j + nj))],
        out_specs=pl.BlockSpec((tm, tf), lambda i, j: (i, j)),
        out_shape=jax.ShapeDtypeStruct((t, f), BF16),
        scratch_shapes=[pltpu.VMEM((tm + SUBLANES, tf), F32),
                        pltpu.VMEM((tm + SUBLANES, tf), F32),
                        pltpu.VMEM((nj, SUBLANES, tf), F32),
                        pltpu.VMEM((nj, SUBLANES, tf), F32)],
        compiler_params=_params("arbitrary", "arbitrary"),
        name="ffn_up",
    )(hn, w_up, w_up, conv_w, conv_w, conv_b.reshape(1, -1), conv_b.reshape(1, -1))


def _gdn_qkv_body(tiles_per_seq, n_q_tiles, n_qk_tiles, x_ref, w_ref, cw_ref, o_ref, buf, carry):
    i = pl.program_id(0)
    j = pl.program_id(1)
    first = (i % tiles_per_seq) == 0
    u = jnp.dot(x_ref[...], w_ref[...], preferred_element_type=F32)
    y = _causal_conv(u, buf, carry, j, first, cw_ref, 4)
    y = y * _sigmoid(y)

    @pl.when(j >= n_qk_tiles)
    def _():
        o_ref[...] = y.astype(o_ref.dtype)

    @pl.when(j < n_qk_tiles)
    def _():
        scale = jnp.where(j < n_q_tiles, GDN_HEAD_DIM ** -0.5, 1.0).astype(F32)
        for g in range(y.shape[1] // GDN_HEAD_DIM):
            blk = y[:, g * GDN_HEAD_DIM:(g + 1) * GDN_HEAD_DIM]
            nrm = lax.rsqrt(jnp.sum(blk * blk, axis=-1, keepdims=True) + L2_EPS) * scale
            o_ref[:, g * GDN_HEAD_DIM:(g + 1) * GDN_HEAD_DIM] = (blk * nrm).astype(o_ref.dtype)


def gdn_qkv(hn, w_in, conv_w, seq, tm=1024, tn=512):
    t, d = hn.shape
    nj = GDN_CONV_DIM // tn
    return pl.pallas_call(
        functools.partial(_gdn_qkv_body, seq // tm, GDN_KEY_DIM // tn, 2 * GDN_KEY_DIM // tn),
        grid=(t // tm, nj),
        in_specs=[pl.BlockSpec((tm, d), lambda i, j: (i, 0)),
                  pl.BlockSpec((d, tn), lambda i, j: (0, j)),
                  pl.BlockSpec((4, tn), lambda i, j: (0, j))],
        out_specs=pl.BlockSpec((tm, tn), lambda i, j: (i, j)),
        out_shape=jax.ShapeDtypeStruct((t, GDN_CONV_DIM), BF16),
        scratch_shapes=[pltpu.VMEM((tm + SUBLANES, tn), F32),
                        pltpu.VMEM((nj, SUBLANES, tn), F32)],
        compiler_params=_params("arbitrary", "arbitrary"),
        name="gdn_qkv",
    )(hn, w_in, conv_w)


def _gdn_gate_body(x_ref, w_ref, alog_ref, dtb_ref, beta_ref, gc_ref):
    tm = x_ref.shape[0]
    nh = GDN_V_HEADS
    ba = jnp.dot(x_ref[...], w_ref[...], preferred_element_type=F32)
    b = ba[:, 0:nh]
    a = ba[:, nh:2 * nh] + dtb_ref[...]
    beta_ref[...] = _sigmoid(b)
    softplus = jnp.maximum(a, 0.0) + jnp.log(1.0 + jnp.exp(-jnp.abs(a)))
    g = -jnp.exp(alog_ref[...]) * softplus
    row = lax.broadcasted_iota(jnp.int32, (tm, tm), 0)
    col = lax.broadcasted_iota(jnp.int32, (tm, tm), 1)
    same_chunk = (row // GDN_CHUNK) == (col // GDN_CHUNK)
    tri = jnp.where(jnp.logical_and(same_chunk, col <= row), 1.0, 0.0).astype(F32)
    gc_ref[...] = jnp.dot(tri, g, preferred_element_type=F32, precision=lax.Precision.HIGHEST)


def gdn_gates(hn, w_ba, a_log, dt_bias, tm=256):
    t, d = hn.shape
    nh = GDN_V_HEADS
    return pl.pallas_call(
        _gdn_gate_body,
        grid=(t // tm,),
        in_specs=[pl.BlockSpec((tm, d), lambda i: (i, 0)),
                  pl.BlockSpec((d, 128), lambda i: (0, 0)),
                  pl.BlockSpec((1, nh), lambda i: (0, 0)),
                  pl.BlockSpec((1, nh), lambda i: (0, 0))],
        out_specs=[pl.BlockSpec((tm, nh), lambda i: (i, 0)),
                   pl.BlockSpec((tm, nh), lambda i: (i, 0))],
        out_shape=[jax.ShapeDtypeStruct((t, nh), F32),
                   jax.ShapeDtypeStruct((t, nh), F32)],
        compiler_params=_params("parallel"),
        name="gdn_gates",
    )(hn, w_ba, a_log.reshape(1, nh), dt_bias.reshape(1, nh))


def _dot_t(a, b):
    return lax.dot_general(a, b, (((1,), (1,)), ((), ())), preferred_element_type=F32)


GDN_GROUP = 4
GDN_GT = GDN_GROUP * GDN_CHUNK


def _group_masks():
    row = lax.broadcasted_iota(jnp.int32, (GDN_GT, GDN_GT), 0)
    col = lax.broadcasted_iota(jnp.int32, (GDN_GT, GDN_GT), 1)
    same = (row // GDN_CHUNK) == (col // GDN_CHUNK)
    return same, jnp.logical_and(same, col <= row), jnp.logical_and(same, col < row)


def _fold(x):
    c = GDN_CHUNK
    out = x[0:c]
    for i in range(1, GDN_GROUP):
        out = out + x[i * c:(i + 1) * c]
    return out


def _unfold(xcat, same):
    return jnp.where(same, jnp.concatenate([xcat] * GDN_GROUP, axis=0), jnp.zeros((), xcat.dtype))


def _gdn_tinv_body(k_ref, gc_ref, gct_ref, beta_ref, t_ref):
    c = GDN_CHUNK
    tt = k_ref.shape[0]
    same, tril, strict = _group_masks()
    eye_cat = jnp.where(
        jnp.bitwise_and(lax.broadcasted_iota(jnp.int32, (c, GDN_GT), 1), c - 1)
        == lax.broadcasted_iota(jnp.int32, (c, GDN_GT), 0), 1.0, 0.0).astype(F32)
    for g in range(tt // GDN_GT):
        r = slice(g * GDN_GT, (g + 1) * GDN_GT)
        k4 = k_ref[r, :]
        gram = _dot_t(k4, k4)
        for j in range(2):
            gcol = gc_ref[r, j:j + 1]
            grow = gct_ref[j:j + 1, r]
            bcol = beta_ref[r, j:j + 1]
            dec = jnp.exp(jnp.where(tril, gcol - grow, NEG_BIG))
            afull = jnp.where(strict, -(bcol * gram) * dec, 0.0)
            acat = _fold(afull)
            pcat = eye_cat + acat
            pw = jnp.dot(acat.astype(BF16), afull.astype(BF16), preferred_element_type=F32)
            for _ in range(GDN_GROUP):
                rhs = _unfold(pw.astype(BF16), same)
                lhs = jnp.concatenate([pw, pcat], axis=0).astype(BF16)
                out = jnp.dot(lhs, rhs, preferred_element_type=F32)
                pw = out[0:c]
                pcat = pcat + out[c:2 * c]
            rhs = _unfold(pw.astype(BF16), same)
            pcat = pcat + jnp.dot(pcat.astype(BF16), rhs, preferred_element_type=F32)
            t_ref[j, g] = pcat.astype(t_ref.dtype)


def gdn_tinv(qkv, gc, gct, beta, batch, seq, tt=1024):
    tt = min(tt, seq)
    nt = seq // tt
    hk = GDN_QK_HEADS
    t = qkv.shape[0]
    ng = tt // GDN_GT
    return pl.pallas_call(
        _gdn_tinv_body,
        grid=(batch, hk, nt),
        in_specs=[pl.BlockSpec((tt, 128), lambda b, h, n: (b * nt + n, hk + h)),
                  pl.BlockSpec((None, tt, 2), lambda b, h, n: (h, b * nt + n, 0)),
                  pl.BlockSpec((None, 2, tt), lambda b, h, n: (h, 0, b * nt + n)),
                  pl.BlockSpec((None, tt, 2), lambda b, h, n: (h, b * nt + n, 0))],
        out_specs=pl.BlockSpec((2, ng, GDN_CHUNK, GDN_GT), lambda b, h, n: (h, b * nt + n, 0, 0)),
        out_shape=jax.ShapeDtypeStruct((GDN_V_HEADS, t // GDN_GT, GDN_CHUNK, GDN_GT), BF16),
        compiler_params=_params("parallel", "parallel", "parallel"),
        name="gdn_tinv",
    )(qkv, gc, gct, beta)


def _gdn_scan_body(q_ref, k_ref, v_ref, z_ref, t_ref, gc_ref, gct_ref, beta_ref, nw_ref, o_ref, s_ref):
    c = GDN_CHUNK
    dh = GDN_HEAD_DIM
    n_qk = q_ref.shape[1] // dh

    @pl.when(pl.program_id(2) == 0)
    def _():
        s_ref[...] = jnp.zeros(s_ref.shape, F32)

    same, tril, _ = _group_masks()
    for a in range(n_qk):
        q4 = q_ref[:, a * dh:(a + 1) * dh]
        k4 = k_ref[:, a * dh:(a + 1) * dh]
        qf = q4.astype(F32)
        kf = k4.astype(F32)
        qk = _dot_t(q4, k4)
        for j in range(2):
            hv = 2 * a + j
            cols = slice(hv * dh, (hv + 1) * dh)
            gcol = gc_ref[a, :, j:j + 1]
            grow = gct_ref[a, j:j + 1, :]
            bcol = beta_ref[a, :, j:j + 1]
            dec = jnp.exp(jnp.where(tril, gcol - grow, NEG_BIG))
            attn = (qk * dec).astype(BF16)
            tfull = _unfold(t_ref[hv, 0], same)
            egc = jnp.exp(gcol)
            vb = (v_ref[:, cols].astype(F32) * bcol).astype(BF16)
            kb = (kf * (bcol * egc)).astype(BF16)
            uw = jnp.dot(tfull, jnp.concatenate([vb, kb], axis=1), preferred_element_type=F32)
            u = uw[:, 0:dh]
            w = uw[:, dh:2 * dh].astype(BF16)
            glast = jnp.concatenate(
                [jnp.broadcast_to(gcol[(i + 1) * c - 1:(i + 1) * c, :], (c, 1)) for i in range(GDN_GROUP)],
                axis=0)
            q_dec = (qf * egc).astype(BF16)
            k_dec = (kf * jnp.exp(glast - gcol)).astype(BF16)
            state = s_ref[hv]
            v_new, o_inter = [], []
            for i in range(GDN_GROUP):
                rc = slice(i * c, (i + 1) * c)
                sb = state.astype(BF16)
                vn = u[rc] - jnp.dot(w[rc], sb, preferred_element_type=F32)
                vnb = vn.astype(BF16)
                o_inter.append(jnp.dot(q_dec[rc], sb, preferred_element_type=F32))
                state = state * jnp.exp(gcol[(i + 1) * c - 1:(i + 1) * c, :]) + lax.dot_general(
                    k_dec[rc], vnb, (((0,), (0,)), ((), ())), preferred_element_type=F32)
                v_new.append(vnb)
            s_ref[hv] = state
            o = jnp.concatenate(o_inter, axis=0) + jnp.dot(
                attn, jnp.concatenate(v_new, axis=0), preferred_element_type=F32)
            zf = z_ref[:, cols].astype(F32)
            o_ref[:, cols] = (_rms(o, nw_ref[...]) * (zf * _sigmoid(zf))).astype(o_ref.dtype)


def gdn_scan(qkv, z, tcat, gc, gct, beta, out_norm, batch, seq, hg=2):
    t = qkv.shape[0]
    tt = GDN_GT
    nt = seq // tt
    dh = GDN_HEAD_DIM
    qk_blocks = GDN_KEY_DIM // (hg * dh)
    return pl.pallas_call(
        _gdn_scan_body,
        grid=(batch, GDN_QK_HEADS // hg, nt),
        in_specs=[pl.BlockSpec((tt, hg * dh), lambda b, h, n: (b * nt + n, h)),
                  pl.BlockSpec((tt, hg * dh), lambda b, h, n: (b * nt + n, qk_blocks + h)),
                  pl.BlockSpec((tt, 2 * hg * dh), lambda b, h, n: (b * nt + n, qk_blocks + h)),
                  pl.BlockSpec((tt, 2 * hg * dh), lambda b, h, n: (b * nt + n, h)),
                  pl.BlockSpec((2 * hg, 1, GDN_CHUNK, GDN_GT), lambda b, h, n: (h, b * nt + n, 0, 0)),
                  pl.BlockSpec((hg, tt, 2), lambda b, h, n: (h, b * nt + n, 0)),
                  pl.BlockSpec((hg, 2, tt), lambda b, h, n: (h, 0, b * nt + n)),
                  pl.BlockSpec((hg, tt, 2), lambda b, h, n: (h, b * nt + n, 0)),
                  pl.BlockSpec((1, dh), lambda b, h, n: (0, 0))],
        out_specs=pl.BlockSpec((tt, 2 * hg * dh), lambda b, h, n: (b * nt + n, h)),
        out_shape=jax.ShapeDtypeStruct((t, GDN_VALUE_DIM), BF16),
        scratch_shapes=[pltpu.VMEM((2 * hg, dh, dh), F32)],
        compiler_params=_params("parallel", "parallel", "arbitrary"),
        name="gdn_scan",
    )(qkv, qkv, qkv, z, tcat, gc, gct, beta, out_norm.reshape(1, dh))


def _rope_table_body(pos_ref, freq_ref, sign_ref, cc_ref, ss_ref):
    ang = pos_ref[...].astype(F32) * freq_ref[...]
    cc_ref[...] = jnp.cos(ang)
    ss_ref[...] = jnp.sin(ang) * sign_ref[...]


def rope_tables(positions, tm=2048):
    t = positions.size
    half = MLA_ROPE_DIM // 2
    inv_freq = ROPE_THETA ** (-jnp.arange(0, MLA_ROPE_DIM, 2, dtype=F32) / MLA_ROPE_DIM)
    freq = jnp.concatenate([inv_freq, inv_freq]).reshape(1, MLA_ROPE_DIM)
    sign = jnp.concatenate([-jnp.ones((half,), F32), jnp.ones((half,), F32)]).reshape(1, MLA_ROPE_DIM)
    tm = min(tm, t)
    return pl.pallas_call(
        _rope_table_body,
        grid=(t // tm,),
        in_specs=[pl.BlockSpec((tm, 1), lambda i: (i, 0)),
                  pl.BlockSpec((1, MLA_ROPE_DIM), lambda i: (0, 0)),
                  pl.BlockSpec((1, MLA_ROPE_DIM), lambda i: (0, 0))],
        out_specs=[pl.BlockSpec((tm, MLA_ROPE_DIM), lambda i: (i, 0)),
                   pl.BlockSpec((tm, MLA_ROPE_DIM), lambda i: (i, 0))],
        out_shape=[jax.ShapeDtypeStruct((t, MLA_ROPE_DIM), F32),
                   jax.ShapeDtypeStruct((t, MLA_ROPE_DIM), F32)],
        compiler_params=_params("parallel"),
        name="rope_tables",
    )(positions.reshape(t, 1), freq, sign)


def _mla_proj_body(x_ref, nw_ref, win_ref, qnw_ref, kvnw_ref, wq_ref, wkv_ref, cc_ref, ss_ref,
                   q_ref, k_ref, v_ref):
    nope, rope = MLA_NOPE_DIM, MLA_ROPE_DIM
    hw = nope + 2 * rope
    xn = _rms(x_ref[...], nw_ref[...]).astype(BF16)
    c = jnp.dot(xn, win_ref[...], preferred_element_type=F32)
    cq = _rms(c[:, 0:MLA_Q_RANK], qnw_ref[...]).astype(BF16)
    ckv = _rms(c[:, MLA_Q_RANK:MLA_Q_RANK + MLA_KV_RANK], kvnw_ref[...]).astype(BF16)
    cc = cc_ref[...]
    ss = ss_ref[...]
    r0 = MLA_Q_RANK + MLA_KV_RANK
    kr = (c[:, r0:r0 + rope] * cc + c[:, r0 + rope:r0 + 2 * rope] * ss).astype(BF16)
    scale = MLA_QK_DIM ** -0.5
    for h in range(MLA_HEADS):
        aq = jnp.dot(cq, wq_ref[:, h * hw:(h + 1) * hw], preferred_element_type=F32)
        q_ref[h, :, 0:nope] = (aq[:, 0:nope] * scale).astype(BF16)
        qr = aq[:, nope:nope + rope] * cc + aq[:, nope + rope:nope + 2 * rope] * ss
        q_ref[h, :, nope:nope + rope] = (qr * scale).astype(BF16)
        akv = jnp.dot(ckv, wkv_ref[:, h * 256:(h + 1) * 256], preferred_element_type=F32)
        k_ref[h, :, 0:nope] = akv[:, 0:nope].astype(BF16)
        k_ref[h, :, nope:nope + rope] = kr
        v_ref[h] = akv[:, nope:nope + MLA_V_DIM].astype(BF16)


def _swap_halves(w):
    half = w.shape[-1] // 2
    return jnp.concatenate([w[..., half:], w[..., :half]], axis=-1)


def mla_proj(x, norm_w, w_in, q_norm, kv_norm, w_uq, w_ukv, cc, ss, tm=256):
    t, d = x.shape
    nh, nope, rope = MLA_HEADS, MLA_NOPE_DIM, MLA_ROPE_DIM
    r0 = MLA_Q_RANK + MLA_KV_RANK
    w_kr = w_in[:, r0:r0 + rope]
    w_in_ext = jnp.concatenate([w_in[:, :r0], w_kr, _swap_halves(w_kr)], axis=1).astype(BF16)
    wq = w_uq.reshape(MLA_Q_RANK, nh, MLA_QK_DIM)
    wq_ext = jnp.concatenate([wq, _swap_halves(wq[..., nope:])], axis=-1)
    hw = nope + 2 * rope
    wq_ext = wq_ext.reshape(MLA_Q_RANK, nh * hw).astype(BF16)
    wkv = w_ukv.astype(BF16)
    const = lambda i: (0, 0)
    return pl.pallas_call(
        _mla_proj_body,
        grid=(t // tm,),
        in_specs=[pl.BlockSpec((tm, d), lambda i: (i, 0)),
                  pl.BlockSpec((1, d), const),
                  pl.BlockSpec(w_in_ext.shape, const),
                  pl.BlockSpec((1, MLA_Q_RANK), const),
                  pl.BlockSpec((1, MLA_KV_RANK), const),
                  pl.BlockSpec(wq_ext.shape, const),
                  pl.BlockSpec(wkv.shape, const),
                  pl.BlockSpec((tm, rope), lambda i: (i, 0)),
                  pl.BlockSpec((tm, rope), lambda i: (i, 0))],
        out_specs=[pl.BlockSpec((nh, tm, MLA_QK_DIM), lambda i: (0, i, 0)),
                   pl.BlockSpec((nh, tm, MLA_QK_DIM), lambda i: (0, i, 0)),
                   pl.BlockSpec((nh, tm, MLA_V_DIM), lambda i: (0, i, 0))],
        out_shape=[jax.ShapeDtypeStruct((nh, t, MLA_QK_DIM), BF16),
                   jax.ShapeDtypeStruct((nh, t, MLA_QK_DIM), BF16),
                   jax.ShapeDtypeStruct((nh, t, MLA_V_DIM), BF16)],
        compiler_params=_params("parallel"),
        name="mla_proj",
    )(x, norm_w.reshape(1, d), w_in_ext, q_norm.reshape(1, -1), kv_norm.reshape(1, -1),
      wq_ext, wkv, cc, ss)


def _attn_body(q_ref, k_ref, v_ref, o_ref, m_ref, l_ref, acc_ref):
    tq = q_ref.shape[0]
    qi = pl.program_id(2)
    q = q_ref[...]
    m_ref[...] = jnp.full(m_ref.shape, NEG_BIG, F32)
    l_ref[...] = jnp.zeros(l_ref.shape, F32)
    acc_ref[...] = jnp.zeros(acc_ref.shape, F32)

    def step(kb, masked):
        start = pl.multiple_of(kb * tq, tq)
        k = k_ref[pl.ds(start, tq), :]
        v = v_ref[pl.ds(start, tq), :]
        s = _dot_t(q, k)
        if masked:
            row = lax.broadcasted_iota(jnp.int32, (tq, tq), 0)
            col = lax.broadcasted_iota(jnp.int32, (tq, tq), 1)
            s = jnp.where(col <= row, s, NEG_BIG)
        m_prev = m_ref[...]
        m_new = jnp.maximum(m_prev, jnp.max(s, axis=-1, keepdims=True))
        alpha = jnp.exp(m_prev - m_new)
        p = jnp.exp(s - m_new)
        l_ref[...] = alpha * l_ref[...] + jnp.sum(p, axis=-1, keepdims=True)
        acc_ref[...] = alpha * acc_ref[...] + jnp.dot(p.astype(BF16), v, preferred_element_type=F32)
        m_ref[...] = m_new

    def full_step(kb, carry):
        step(kb, False)
        return carry

    lax.fori_loop(0, qi, full_step, 0)
    step(qi, True)
    o_ref[...] = (acc_ref[...] / l_ref[...]).astype(o_ref.dtype)


def attention(q, k, v, batch, seq, tq=512):
    nh, t, _ = q.shape
    nq = seq // tq
    return pl.pallas_call(
        _attn_body,
        grid=(batch, nh, nq),
        in_specs=[pl.BlockSpec((None, tq, MLA_QK_DIM), lambda b, h, i: (h, b * nq + i, 0)),
                  pl.BlockSpec((None, seq, MLA_QK_DIM), lambda b, h, i: (h, b, 0)),
                  pl.BlockSpec((None, seq, MLA_V_DIM), lambda b, h, i: (h, b, 0))],
        out_specs=pl.BlockSpec((tq, MLA_V_DIM), lambda b, h, i: (b * nq + i, h)),
        out_shape=jax.ShapeDtypeStruct((t, nh * MLA_V_DIM), BF16),
        scratch_shapes=[pltpu.VMEM((tq, 1), F32),
                        pltpu.VMEM((tq, 1), F32),
                        pltpu.VMEM((tq, MLA_V_DIM), F32)],
        compiler_params=_params("parallel", "parallel", "arbitrary"),
        name="mla_attention",
    )(q, k, v)


def _ffn_layer(h, norm_w, w_up, conv_w, conv_b, w_down, seq):
    hn = rmsnorm(h, norm_w, BF16)
    act = ffn_up(hn, w_up.astype(BF16), conv_w, conv_b, seq)
    return matmul(act, w_down.astype(BF16), n=w_down.shape[1], res=h, tm=1024, tn=512,
                  name="ffn_down")


def _mla_layer(h, positions, norm_w, w_in, q_norm, kv_norm, w_uq, w_ukv, w_o, batch, seq):
    cc, ss = rope_tables(positions)
    q, k, v = mla_proj(h, norm_w, w_in, q_norm, kv_norm, w_uq, w_ukv, cc, ss)
    o = attention(q, k, v, batch, seq)
    return matmul(o, w_o.astype(BF16), n=w_o.shape[1], res=h, tm=1024, tn=1024, name="mla_out")


def _gdn_layer(h, norm_w, w_in, conv_w, a_log, dt_bias, out_norm, w_o, batch, seq):
    t = h.shape[0]
    hn = rmsnorm(h, norm_w, BF16)
    w_in_b = w_in.astype(BF16)
    qkv = gdn_qkv(hn, w_in_b, conv_w, seq)
    z = matmul(hn, w_in_b, n=GDN_VALUE_DIM, b_col0=GDN_CONV_DIM, out_dtype=BF16, tm=1024, tn=512,
               name="gdn_z")
    w_ba = jnp.pad(w_in_b[:, GDN_CONV_DIM + GDN_VALUE_DIM:], ((0, 0), (0, 128 - 2 * GDN_V_HEADS)))
    beta, gc = gdn_gates(hn, w_ba, a_log, dt_bias)
    hk = GDN_QK_HEADS
    gc3 = gc.reshape(t, hk, 2)
    gc_col = gc3.transpose(1, 0, 2)
    gc_row = gc3.transpose(1, 2, 0)
    beta_col = beta.reshape(t, hk, 2).transpose(1, 0, 2)
    tcat = gdn_tinv(qkv, gc_col, gc_row, beta_col, batch, seq)
    o = gdn_scan(qkv, z, tcat, gc_col, gc_row, beta_col, out_norm, batch, seq)
    return matmul(o, w_o.astype(BF16), n=w_o.shape[1], res=h, tm=1024, tn=1024, name="gdn_out")


def kernel(x, positions, mla_norm, mla_w_in, mla_q_norm, mla_kv_norm, mla_w_uq, mla_w_ukv, mla_w_o,
           gdn_norm, gdn_w_in, gdn_conv_w, gdn_a_log, gdn_dt_bias, gdn_out_norm, gdn_w_o, ffn_norm,
           ffn_w_up, ffn_conv_w, ffn_conv_b, ffn_w_down, final_norm):
    batch, seq, d = x.shape
    h = x.reshape(batch * seq, d)
    depth = ffn_norm.shape[0]
    for i in range(depth):
        j = i // 2
        if i % 2 == 0:
            h = _mla_layer(h, positions, mla_norm[j], mla_w_in[j], mla_q_norm[j], mla_kv_norm[j],
                           mla_w_uq[j], mla_w_ukv[j], mla_w_o[j], batch, seq)
        else:
            h = _gdn_layer(h, gdn_norm[j], gdn_w_in[j], gdn_conv_w[j], gdn_a_log[j], gdn_dt_bias[j],
                           gdn_out_norm[j], gdn_w_o[j], batch, seq)
        h = _ffn_layer(h, ffn_norm[i], ffn_w_up[i], ffn_conv_w[i], ffn_conv_b[i], ffn_w_down[i], seq)
    return rmsnorm(h, final_norm, x.dtype).reshape(batch, seq, d)
```

```python
import functools
import math

import jax
import jax.numpy as jnp
from jax import lax
from jax.experimental import pallas as pl
from jax.experimental.pallas import tpu as pltpu

F32 = jnp.float32
BF16 = jnp.bfloat16

RMS_EPS = 1e-6
L2_EPS = 1e-6

MLA_HEADS = 16
MLA_Q_RANK = 512
MLA_KV_RANK = 512
MLA_NOPE_DIM = 128
MLA_ROPE_DIM = 64
MLA_V_DIM = 128
MLA_QK_DIM = MLA_NOPE_DIM + MLA_ROPE_DIM
ROPE_THETA = 10000.0

GDN_QK_HEADS = 16
GDN_V_HEADS = 32
GDN_HEAD_DIM = 128
GDN_KEY_DIM = GDN_QK_HEADS * GDN_HEAD_DIM
GDN_VALUE_DIM = GDN_V_HEADS * GDN_HEAD_DIM
GDN_CONV_DIM = 2 * GDN_KEY_DIM + GDN_VALUE_DIM
GDN_CHUNK = 64

VMEM_LIMIT_BYTES = 56 * 1024 * 1024
SUBLANES = 8
LANES = 128
MXU_WIDTH = 256
NEG_BIG = -1e30
LOG2_E = math.log2(math.e)

ATTN_BLOCK = 512
ATTN_HEADS = 2


def _params(*sem):
    return pltpu.CompilerParams(dimension_semantics=sem, vmem_limit_bytes=VMEM_LIMIT_BYTES)


def _rms(x, w):
    return x * lax.rsqrt(jnp.mean(x * x, axis=-1, keepdims=True) + RMS_EPS) * w


def _sigmoid(x):
    return 1.0 / (1.0 + jnp.exp(-x))


def _rmsnorm_body(x_ref, w_ref, o_ref):
    o_ref[...] = _rms(x_ref[...], w_ref[...]).astype(o_ref.dtype)


def rmsnorm(x, w, out_dtype, tm=512):
    t, d = x.shape
    return pl.pallas_call(
        _rmsnorm_body,
        grid=(t // tm,),
        in_specs=[pl.BlockSpec((tm, d), lambda i: (i, 0)),
                  pl.BlockSpec((1, d), lambda i: (0, 0))],
        out_specs=pl.BlockSpec((tm, d), lambda i: (i, 0)),
        out_shape=jax.ShapeDtypeStruct((t, d), out_dtype),
        compiler_params=_params("parallel"),
        name="rmsnorm",
    )(x, w.reshape(1, d))


def _matmul_body(a_ref, b_ref, o_ref):
    o_ref[...] = jnp.dot(a_ref[...], b_ref[...], preferred_element_type=F32).astype(o_ref.dtype)


def _matmul_res_body(a_ref, b_ref, r_ref, o_ref):
    acc = jnp.dot(a_ref[...], b_ref[...], preferred_element_type=F32)
    o_ref[...] = (r_ref[...] + acc).astype(o_ref.dtype)


def matmul(a, b, *, n, b_col0=0, res=None, out_dtype=F32, tm, tn, name):
    m, k = a.shape
    joff = b_col0 // tn
    in_specs = [pl.BlockSpec((tm, k), lambda i, j: (i, 0)),
                pl.BlockSpec((k, tn), lambda i, j: (0, j + joff))]
    args = [a, b]
    body = _matmul_body
    if res is not None:
        in_specs.append(pl.BlockSpec((tm, tn), lambda i, j: (i, j)))
        args.append(res)
        body = _matmul_res_body
    return pl.pallas_call(
        body,
        grid=(m // tm, n // tn),
        in_specs=in_specs,
        out_specs=pl.BlockSpec((tm, tn), lambda i, j: (i, j)),
        out_shape=jax.ShapeDtypeStruct((m, n), out_dtype),
        compiler_params=_params("parallel", "parallel"),
        name=name,
    )(*args)


def _init_carry(i, j, *carry_refs):
    @pl.when(jnp.logical_and(i == 0, j == 0))
    def _():
        for ref in carry_refs:
            ref[...] = jnp.zeros(ref.shape, F32)


ROW_CHUNK = 256


def _conv_head(buf_ref, carry_ref, j, first):
    buf_ref[0:SUBLANES, :] = jnp.where(first, 0.0, carry_ref[j])


def _conv_rows(buf_ref, cw_ref, width, r0, nrows):
    base = SUBLANES + r0
    y = cw_ref[width - 1:width, :] * buf_ref[base:base + nrows, :]
    for d in range(1, width):
        y = y + cw_ref[width - 1 - d:width - d, :] * buf_ref[base - d:base - d + nrows, :]
    return y


def _chunked(tm, chunk, matmul_chunk, tail_chunk):
    n = tm // chunk
    for r in range(n):
        matmul_chunk(r * chunk, chunk)
        if r >= 1:
            tail_chunk((r - 1) * chunk, chunk)
    tail_chunk((n - 1) * chunk, chunk)


def _ffn_up_body(tiles_per_seq, x_ref, wg_ref, wu_ref, cwg_ref, cwu_ref, bg_ref, bu_ref, o_ref,
                 gbuf, ubuf, gcarry, ucarry):
    i = pl.program_id(0)
    j = pl.program_id(1)
    tm = x_ref.shape[0]
    first = (i % tiles_per_seq) == 0
    _init_carry(i, j, gcarry, ucarry)
    _conv_head(gbuf, gcarry, j, first)
    _conv_head(ubuf, ucarry, j, first)

    def matmul_chunk(r0, nr):
        x = x_ref[r0:r0 + nr, :]
        rows = slice(SUBLANES + r0, SUBLANES + r0 + nr)
        gbuf[rows, :] = jnp.dot(x, wg_ref[...], preferred_element_type=F32)
        ubuf[rows, :] = jnp.dot(x, wu_ref[...], preferred_element_type=F32)

    def tail_chunk(r0, nr):
        yg = _conv_rows(gbuf, cwg_ref, 3, r0, nr) + bg_ref[...]
        yu = _conv_rows(ubuf, cwu_ref, 3, r0, nr) + bu_ref[...]
        o_ref[r0:r0 + nr, :] = (yg * _sigmoid(yg) * yu).astype(o_ref.dtype)

    _chunked(tm, tm, matmul_chunk, tail_chunk)
    gcarry[j] = gbuf[tm:tm + SUBLANES, :]
    ucarry[j] = ubuf[tm:tm + SUBLANES, :]


def ffn_up(hn, w_up, conv_w, conv_b, seq, tm=1024, tf=512):
    t, d = hn.shape
    f = w_up.shape[1] // 2
    nj = f // tf
    return pl.pallas_call(
        functools.partial(_ffn_up_body, seq // tm),
        grid=(t // tm, nj),
        in_specs=[pl.BlockSpec((tm, d), lambda i, j: (i, 0)),
                  pl.BlockSpec((d, tf), lambda i, j: (0, j)),
                  pl.BlockSpec((d, tf), lambda i, j: (0, j + nj)),
                  pl.BlockSpec((3, tf), lambda i, j: (0, j)),
                  pl.BlockSpec((3, tf), lambda i, j: (0, j + nj)),
                  pl.BlockSpec((1, tf), lambda i, j: (0, j)),
                  pl.BlockSpec((1, tf), lambda i, j: (0, j + nj))],
        out_specs=pl.BlockSpec((tm, tf), lambda i, j: (i, j)),
        out_shape=jax.ShapeDtypeStruct((t, f), BF16),
        scratch_shapes=[pltpu.VMEM((tm + SUBLANES, tf), F32),
                        pltpu.VMEM((tm + SUBLANES, tf), F32),
                        pltpu.VMEM((nj, SUBLANES, tf), F32),
                        pltpu.VMEM((nj, SUBLANES, tf), F32)],
        compiler_params=_params("arbitrary", "arbitrary"),
        name="ffn_up",
    )(hn, w_up, w_up, conv_w, conv_w, conv_b.reshape(1, -1), conv_b.reshape(1, -1))


def _gdn_qkv_body(tiles_per_seq, n_q_tiles, n_qk_tiles, x_ref, w_ref, cw_ref, o_ref, buf, carry):
    i = pl.program_id(0)
    j = pl.program_id(1)
    tm = x_ref.shape[0]
    first = (i % tiles_per_seq) == 0
    _init_carry(i, j, carry)
    _conv_head(buf, carry, j, first)
    is_qk = j < n_qk_tiles
    scale = jnp.where(j < n_q_tiles, GDN_HEAD_DIM ** -0.5, 1.0).astype(F32)
    dh = GDN_HEAD_DIM

    def matmul_chunk(r0, nr):
        buf[SUBLANES + r0:SUBLANES + r0 + nr, :] = jnp.dot(
            x_ref[r0:r0 + nr, :], w_ref[...], preferred_element_type=F32)

    def tail_chunk(r0, nr):
        y = _conv_rows(buf, cw_ref, 4, r0, nr)
        y = y * _sigmoid(y)
        for g in range(o_ref.shape[1] // dh):
            blk = y[:, g * dh:(g + 1) * dh]
            nrm = lax.rsqrt(jnp.sum(blk * blk, axis=-1, keepdims=True) + L2_EPS) * scale
            nrm = jnp.where(is_qk, nrm, 1.0)
            o_ref[r0:r0 + nr, g * dh:(g + 1) * dh] = (blk * nrm).astype(o_ref.dtype)

    _chunked(tm, ROW_CHUNK, matmul_chunk, tail_chunk)
    carry[j] = buf[tm:tm + SUBLANES, :]


def gdn_qkv(hn, w_in, conv_w, seq, tm=1024, tn=512):
    t, d = hn.shape
    nj = GDN_CONV_DIM // tn
    return pl.pallas_call(
        functools.partial(_gdn_qkv_body, seq // tm, GDN_KEY_DIM // tn, 2 * GDN_KEY_DIM // tn),
        grid=(t // tm, nj),
        in_specs=[pl.BlockSpec((tm, d), lambda i, j: (i, 0)),
                  pl.BlockSpec((d, tn), lambda i, j: (0, j)),
                  pl.BlockSpec((4, tn), lambda i, j: (0, j))],
        out_specs=pl.BlockSpec((tm, tn), lambda i, j: (i, j)),
        out_shape=jax.ShapeDtypeStruct((t, GDN_CONV_DIM), BF16),
        scratch_shapes=[pltpu.VMEM((tm + SUBLANES, tn), F32),
                        pltpu.VMEM((nj, SUBLANES, tn), F32)],
        compiler_params=_params("arbitrary", "arbitrary"),
        name="gdn_qkv",
    )(hn, w_in, conv_w)


def _gdn_gate_body(x_ref, w_ref, alog_ref, dtb_ref, beta_ref, gc_ref):
    tm = x_ref.shape[0]
    nh = GDN_V_HEADS
    ba = jnp.dot(x_ref[...], w_ref[...], preferred_element_type=F32)
    b = ba[:, 0:nh]
    a = ba[:, nh:2 * nh] + dtb_ref[...]
    beta_ref[...] = _sigmoid(b)
    softplus = jnp.maximum(a, 0.0) + jnp.log(1.0 + jnp.exp(-jnp.abs(a)))
    g = -jnp.exp(alog_ref[...]) * softplus
    row = lax.broadcasted_iota(jnp.int32, (tm, tm), 0)
    col = lax.broadcasted_iota(jnp.int32, (tm, tm), 1)
    same_chunk = (row // GDN_CHUNK) == (col // GDN_CHUNK)
    tri = jnp.where(jnp.logical_and(same_chunk, col <= row), 1.0, 0.0).astype(F32)
    gc_ref[...] = jnp.dot(tri, g, preferred_element_type=F32, precision=lax.Precision.HIGHEST)


def gdn_gates(hn, w_ba, a_log, dt_bias, tm=256):
    t, d = hn.shape
    nh = GDN_V_HEADS
    return pl.pallas_call(
        _gdn_gate_body,
        grid=(t // tm,),
        in_specs=[pl.BlockSpec((tm, d), lambda i: (i, 0)),
                  pl.BlockSpec((d, 128), lambda i: (0, 0)),
                  pl.BlockSpec((1, nh), lambda i: (0, 0)),
                  pl.BlockSpec((1, nh), lambda i: (0, 0))],
        out_specs=[pl.BlockSpec((tm, nh), lambda i: (i, 0)),
                   pl.BlockSpec((tm, nh), lambda i: (i, 0))],
        out_shape=[jax.ShapeDtypeStruct((t, nh), F32),
                   jax.ShapeDtypeStruct((t, nh), F32)],
        compiler_params=_params("parallel"),
        name="gdn_gates",
    )(hn, w_ba, a_log.reshape(1, nh), dt_bias.reshape(1, nh))


def _dot_t(a, b):
    return lax.dot_general(a, b, (((1,), (1,)), ((), ())), preferred_element_type=F32)


GDN_GROUP = 4
GDN_GT = GDN_GROUP * GDN_CHUNK


def _group_masks():
    row = lax.broadcasted_iota(jnp.int32, (GDN_GT, GDN_GT), 0)
    col = lax.broadcasted_iota(jnp.int32, (GDN_GT, GDN_GT), 1)
    same = (row // GDN_CHUNK) == (col // GDN_CHUNK)
    return same, jnp.logical_and(same, col <= row), jnp.logical_and(same, col < row)


def _fold(x):
    c = GDN_CHUNK
    out = x[0:c]
    for i in range(1, GDN_GROUP):
        out = out + x[i * c:(i + 1) * c]
    return out


def _unfold(xcat, same):
    return jnp.where(same, jnp.concatenate([xcat] * GDN_GROUP, axis=0), jnp.zeros((), xcat.dtype))


def _gdn_tinv_body(k_ref, gc_ref, gct_ref, beta_ref, t_ref):
    c = GDN_CHUNK
    tt = k_ref.shape[0]
    same, tril, strict = _group_masks()
    eye_cat = jnp.where(
        jnp.bitwise_and(lax.broadcasted_iota(jnp.int32, (c, GDN_GT), 1), c - 1)
        == lax.broadcasted_iota(jnp.int32, (c, GDN_GT), 0), 1.0, 0.0).astype(F32)
    chains = [(g, j) for g in range(tt // GDN_GT) for j in range(2)]
    grams = []
    for g in range(tt // GDN_GT):
        k4 = k_ref[g * GDN_GT:(g + 1) * GDN_GT, :]
        grams.append(_dot_t(k4, k4))
    pws, pcats = [], []
    for g, j in chains:
        r = slice(g * GDN_GT, (g + 1) * GDN_GT)
        gcol = gc_ref[r, j:j + 1]
        grow = gct_ref[j:j + 1, r]
        bcol = beta_ref[r, j:j + 1]
        dec = jnp.exp(jnp.where(tril, gcol - grow, NEG_BIG))
        afull = jnp.where(strict, -(bcol * grams[g]) * dec, 0.0)
        acat = _fold(afull)
        pcats.append(eye_cat + acat)
        pws.append(jnp.dot(acat.astype(BF16), afull.astype(BF16), preferred_element_type=F32))
    for _ in range(GDN_GROUP):
        for n in range(len(chains)):
            rhs = _unfold(pws[n].astype(BF16), same)
            lhs = jnp.concatenate([pws[n], pcats[n]], axis=0).astype(BF16)
            out = jnp.dot(lhs, rhs, preferred_element_type=F32)
            pws[n] = out[0:c]
            pcats[n] = pcats[n] + out[c:2 * c]
    for n, (g, j) in enumerate(chains):
        rhs = _unfold(pws[n].astype(BF16), same)
        tfin = pcats[n] + jnp.dot(pcats[n].astype(BF16), rhs, preferred_element_type=F32)
        t_ref[j, g] = tfin.astype(t_ref.dtype)


def gdn_tinv(qkv, gc, gct, beta, batch, seq, tt=1024):
    tt = min(tt, seq)
    nt = seq // tt
    hk = GDN_QK_HEADS
    t = qkv.shape[0]
    ng = tt // GDN_GT
    return pl.pallas_call(
        _gdn_tinv_body,
        grid=(batch, hk, nt),
        in_specs=[pl.BlockSpec((tt, 128), lambda b, h, n: (b * nt + n, hk + h)),
                  pl.BlockSpec((None, tt, 2), lambda b, h, n: (h, b * nt + n, 0)),
                  pl.BlockSpec((None, 2, tt), lambda b, h, n: (h, 0, b * nt + n)),
                  pl.BlockSpec((None, tt, 2), lambda b, h, n: (h, b * nt + n, 0))],
        out_specs=pl.BlockSpec((2, ng, GDN_CHUNK, GDN_GT), lambda b, h, n: (h, b * nt + n, 0, 0)),
        out_shape=jax.ShapeDtypeStruct((GDN_V_HEADS, t // GDN_GT, GDN_CHUNK, GDN_GT), BF16),
        compiler_params=_params("parallel", "parallel", "parallel"),
        name="gdn_tinv",
    )(qkv, gc, gct, beta)


def _gdn_scan_body(q_ref, k_ref, v_ref, z_ref, t_ref, gc_ref, gct_ref, beta_ref, nw_ref, o_ref, s_ref):
    c = GDN_CHUNK
    dh = GDN_HEAD_DIM
    n_qk = q_ref.shape[1] // dh

    @pl.when(pl.program_id(2) == 0)
    def _():
        s_ref[...] = jnp.zeros(s_ref.shape, F32)

    same, tril, _ = _group_masks()
    heads = [(a, j) for a in range(n_qk) for j in range(2)]
    qks, kfs, qfs = [], [], []
    for a in range(n_qk):
        q4 = q_ref[:, a * dh:(a + 1) * dh]
        k4 = k_ref[:, a * dh:(a + 1) * dh]
        qks.append(_dot_t(q4, k4))
        qfs.append(q4.astype(F32))
        kfs.append(k4.astype(F32))
    us, wqs, attns, k_decs, gts, states = [], [], [], [], [], []
    for a, j in heads:
        hv = 2 * a + j
        cols = slice(hv * dh, (hv + 1) * dh)
        gcol = gc_ref[a, :, j:j + 1]
        grow = gct_ref[a, j:j + 1, :]
        bcol = beta_ref[a, :, j:j + 1]
        dec = jnp.exp(jnp.where(tril, gcol - grow, NEG_BIG))
        attns.append((qks[a] * dec).astype(BF16))
        tfull = _unfold(t_ref[hv, 0], same)
        egc = jnp.exp(gcol)
        vb = (v_ref[:, cols].astype(F32) * bcol).astype(BF16)
        kb = (kfs[a] * (bcol * egc)).astype(BF16)
        uw = jnp.dot(tfull, jnp.concatenate([vb, kb], axis=1), preferred_element_type=F32)
        us.append(uw[:, 0:dh])
        w = uw[:, dh:2 * dh].astype(BF16)
        q_dec = (qfs[a] * egc).astype(BF16)
        wqs.append([jnp.concatenate([w[i * c:(i + 1) * c], q_dec[i * c:(i + 1) * c]], axis=0)
                    for i in range(GDN_GROUP)])
        glasts = [gcol[(i + 1) * c - 1:(i + 1) * c, :] for i in range(GDN_GROUP)]
        glast = jnp.concatenate([jnp.broadcast_to(gl, (c, 1)) for gl in glasts], axis=0)
        k_decs.append((kfs[a] * jnp.exp(glast - gcol)).astype(BF16))
        gts.append([jnp.exp(gl) for gl in glasts])
        states.append(s_ref[hv])
    v_new = [[] for _ in heads]
    o_inter = [[] for _ in heads]
    for i in range(GDN_GROUP):
        rc = slice(i * c, (i + 1) * c)
        vnbs = []
        for n in range(len(heads)):
            ws = jnp.dot(wqs[n][i], states[n].astype(BF16), preferred_element_type=F32)
            vnbs.append((us[n][rc] - ws[0:c]).astype(BF16))
            o_inter[n].append(ws[c:2 * c])
        for n in range(len(heads)):
            states[n] = states[n] * gts[n][i] + lax.dot_general(
                k_decs[n][rc], vnbs[n], (((0,), (0,)), ((), ())), preferred_element_type=F32)
            v_new[n].append(vnbs[n])
    for n, (a, j) in enumerate(heads):
        hv = 2 * a + j
        cols = slice(hv * dh, (hv + 1) * dh)
        s_ref[hv] = states[n]
        o = jnp.concatenate(o_inter[n], axis=0) + jnp.dot(
            attns[n], jnp.concatenate(v_new[n], axis=0), preferred_element_type=F32)
        zf = z_ref[:, cols].astype(F32)
        o_ref[:, cols] = (_rms(o, nw_ref[...]) * (zf * _sigmoid(zf))).astype(o_ref.dtype)


def gdn_scan(qkv, z, tcat, gc, gct, beta, out_norm, batch, seq, hg=4):
    t = qkv.shape[0]
    tt = GDN_GT
    nt = seq // tt
    dh = GDN_HEAD_DIM
    qk_blocks = GDN_KEY_DIM // (hg * dh)
    return pl.pallas_call(
        _gdn_scan_body,
        grid=(batch, GDN_QK_HEADS // hg, nt),
        in_specs=[pl.BlockSpec((tt, hg * dh), lambda b, h, n: (b * nt + n, h)),
                  pl.BlockSpec((tt, hg * dh), lambda b, h, n: (b * nt + n, qk_blocks + h)),
                  pl.BlockSpec((tt, 2 * hg * dh), lambda b, h, n: (b * nt + n, qk_blocks + h)),
                  pl.BlockSpec((tt, 2 * hg * dh), lambda b, h, n: (b * nt + n, h)),
                  pl.BlockSpec((2 * hg, 1, GDN_CHUNK, GDN_GT), lambda b, h, n: (h, b * nt + n, 0, 0)),
                  pl.BlockSpec((hg, tt, 2), lambda b, h, n: (h, b * nt + n, 0)),
                  pl.BlockSpec((hg, 2, tt), lambda b, h, n: (h, 0, b * nt + n)),
                  pl.BlockSpec((hg, tt, 2), lambda b, h, n: (h, b * nt + n, 0)),
                  pl.BlockSpec((1, dh), lambda b, h, n: (0, 0))],
        out_specs=pl.BlockSpec((tt, 2 * hg * dh), lambda b, h, n: (b * nt + n, h)),
        out_shape=jax.ShapeDtypeStruct((t, GDN_VALUE_DIM), BF16),
        scratch_shapes=[pltpu.VMEM((2 * hg, dh, dh), F32)],
        compiler_params=_params("parallel", "parallel", "arbitrary"),
        name="gdn_scan",
    )(qkv, qkv, qkv, z, tcat, gc, gct, beta, out_norm.reshape(1, dh))


def _rope_table_body(pos_ref, freq_ref, sign_ref, cc_ref, ss_ref):
    ang = pos_ref[...].astype(F32) * freq_ref[...]
    cc_ref[...] = jnp.cos(ang)
    ss_ref[...] = jnp.sin(ang) * sign_ref[...]


def rope_tables(positions, tm=2048):
    t = positions.size
    half = MLA_ROPE_DIM // 2
    inv_freq = ROPE_THETA ** (-jnp.arange(0, MLA_ROPE_DIM, 2, dtype=F32) / MLA_ROPE_DIM)
    freq = jnp.concatenate([inv_freq, inv_freq]).reshape(1, MLA_ROPE_DIM)
    sign = jnp.concatenate([-jnp.ones((half,), F32), jnp.ones((half,), F32)]).reshape(1, MLA_ROPE_DIM)
    tm = min(tm, t)
    return pl.pallas_call(
        _rope_table_body,
        grid=(t // tm,),
        in_specs=[pl.BlockSpec((tm, 1), lambda i: (i, 0)),
                  pl.BlockSpec((1, MLA_ROPE_DIM), lambda i: (0, 0)),
                  pl.BlockSpec((1, MLA_ROPE_DIM), lambda i: (0, 0))],
        out_specs=[pl.BlockSpec((tm, MLA_ROPE_DIM), lambda i: (i, 0)),
                   pl.BlockSpec((tm, MLA_ROPE_DIM), lambda i: (i, 0))],
        out_shape=[jax.ShapeDtypeStruct((t, MLA_ROPE_DIM), F32),
                   jax.ShapeDtypeStruct((t, MLA_ROPE_DIM), F32)],
        compiler_params=_params("parallel"),
        name="rope_tables",
    )(positions.reshape(t, 1), freq, sign)


def _mla_proj_body(x_ref, nw_ref, win_ref, qnw_ref, kvnw_ref, wq_ref, wkv_ref, cc_ref, ss_ref,
                   q_ref, kt_ref, v_ref):
    nope, rope = MLA_NOPE_DIM, MLA_ROPE_DIM
    hw = nope + 2 * rope
    xn = _rms(x_ref[...], nw_ref[...]).astype(BF16)
    c = jnp.dot(xn, win_ref[...], preferred_element_type=F32)
    cq = _rms(c[:, 0:MLA_Q_RANK], qnw_ref[...]).astype(BF16)
    ckv = _rms(c[:, MLA_Q_RANK:MLA_Q_RANK + MLA_KV_RANK], kvnw_ref[...]).astype(BF16)
    cc = cc_ref[...]
    ss = ss_ref[...]
    r0 = MLA_Q_RANK + MLA_KV_RANK
    kr = c[:, r0:r0 + rope] * cc + c[:, r0 + rope:r0 + 2 * rope] * ss
    scale = MLA_QK_DIM ** -0.5 * LOG2_E
    kr_pad = jnp.concatenate([kr, jnp.zeros_like(kr)], axis=1)
    for h in range(MLA_HEADS):
        aq = jnp.dot(cq, wq_ref[:, h * hw:(h + 1) * hw], preferred_element_type=F32)
        q_ref[h, :, 0:nope] = (aq[:, 0:nope] * scale).astype(BF16)
        qr = aq[:, nope:nope + rope] * cc + aq[:, nope + rope:nope + 2 * rope] * ss
        q_ref[h, :, nope:nope + rope] = (qr * scale).astype(BF16)
        akv = jnp.dot(ckv, wkv_ref[:, h * 256:(h + 1) * 256], preferred_element_type=F32)
        k_t = jnp.concatenate([akv[:, 0:nope], kr_pad], axis=1).T
        kt_ref[h, 0] = k_t[0:MLA_QK_DIM, :].astype(BF16)
        v_ref[h] = akv[:, nope:nope + MLA_V_DIM].astype(BF16)


def _swap_halves(w):
    half = w.shape[-1] // 2
    return jnp.concatenate([w[..., half:], w[..., :half]], axis=-1)


def mla_proj(x, norm_w, w_in, q_norm, kv_norm, w_uq, w_ukv, cc, ss, tm=ATTN_BLOCK):
    t, d = x.shape
    nh, nope, rope = MLA_HEADS, MLA_NOPE_DIM, MLA_ROPE_DIM
    r0 = MLA_Q_RANK + MLA_KV_RANK
    w_kr = w_in[:, r0:r0 + rope]
    w_in_ext = jnp.concatenate([w_in[:, :r0], w_kr, _swap_halves(w_kr)], axis=1).astype(BF16)
    wq = w_uq.reshape(MLA_Q_RANK, nh, MLA_QK_DIM)
    wq_ext = jnp.concatenate([wq, _swap_halves(wq[..., nope:])], axis=-1)
    hw = nope + 2 * rope
    wq_ext = wq_ext.reshape(MLA_Q_RANK, nh * hw).astype(BF16)
    wkv = w_ukv.astype(BF16)
    const = lambda i: (0, 0)
    resident = dict(pipeline_mode=pl.Buffered(1))
    return pl.pallas_call(
        _mla_proj_body,
        grid=(t // tm,),
        in_specs=[pl.BlockSpec((tm, d), lambda i: (i, 0)),
                  pl.BlockSpec((1, d), const),
                  pl.BlockSpec(w_in_ext.shape, const, **resident),
                  pl.BlockSpec((1, MLA_Q_RANK), const),
                  pl.BlockSpec((1, MLA_KV_RANK), const),
                  pl.BlockSpec(wq_ext.shape, const, **resident),
                  pl.BlockSpec(wkv.shape, const, **resident),
                  pl.BlockSpec((tm, rope), lambda i: (i, 0)),
                  pl.BlockSpec((tm, rope), lambda i: (i, 0))],
        out_specs=[pl.BlockSpec((nh, tm, MLA_QK_DIM), lambda i: (0, i, 0)),
                   pl.BlockSpec((nh, 1, MLA_QK_DIM, tm), lambda i: (0, i, 0, 0)),
                   pl.BlockSpec((nh, tm, MLA_V_DIM), lambda i: (0, i, 0))],
        out_shape=[jax.ShapeDtypeStruct((nh, t, MLA_QK_DIM), BF16),
                   jax.ShapeDtypeStruct((nh, t // tm, MLA_QK_DIM, tm), BF16),
                   jax.ShapeDtypeStruct((nh, t, MLA_V_DIM), BF16)],
        compiler_params=_params("parallel"),
        name="mla_proj",
    )(x, norm_w.reshape(1, d), w_in_ext, q_norm.reshape(1, -1), kv_norm.reshape(1, -1),
      wq_ext, wkv, cc, ss)


def _attn_body(q_ref, kt_ref, v_ref, o_ref, m_ref, l_ref, acc_ref):
    nhead, tq, _ = q_ref.shape
    tk = kt_ref.shape[3]
    nlt = tk // LANES
    qi = pl.program_id(2)
    m_ref[...] = jnp.full(m_ref.shape, NEG_BIG, F32)
    l_ref[...] = jnp.zeros(l_ref.shape, F32)
    acc_ref[...] = jnp.zeros(acc_ref.shape, F32)

    def scores(hd, kb):
        return jnp.dot(q_ref[hd], kt_ref[hd, kb], preferred_element_type=F32)

    def update(hd, kb, s, masked):
        start = pl.multiple_of(kb * tk, tk)
        if masked:
            row = lax.broadcasted_iota(jnp.int32, (tq, tk), 0)
            col = lax.broadcasted_iota(jnp.int32, (tq, tk), 1)
            s = jnp.where(col <= row, s, NEG_BIG)
        m_prev = m_ref[hd]
        m_new = jnp.maximum(m_prev, jnp.max(s, axis=-1, keepdims=True))
        alpha = jnp.exp2(m_prev - m_new)
        p = jnp.exp2(s - jnp.concatenate([m_new] * nlt, axis=1))
        psum = p[:, 0:LANES]
        for i in range(1, nlt):
            psum = psum + p[:, i * LANES:(i + 1) * LANES]
        l_ref[hd] = alpha * l_ref[hd] + psum
        acc_ref[hd] = alpha * acc_ref[hd] + jnp.dot(
            p.astype(BF16), v_ref[hd, pl.ds(start, tk), :], preferred_element_type=F32)
        m_ref[hd] = m_new

    def block(kb, masked):
        ss = [scores(hd, kb) for hd in range(nhead)]
        for hd in range(nhead):
            update(hd, kb, ss[hd], masked)

    def full_step(kb, carry):
        block(kb, False)
        return carry

    lax.fori_loop(0, qi, full_step, 0)
    block(qi, True)
    for hd in range(nhead):
        l = jnp.sum(l_ref[hd], axis=-1, keepdims=True)
        o_ref[:, hd * MLA_V_DIM:(hd + 1) * MLA_V_DIM] = (acc_ref[hd] / l).astype(o_ref.dtype)


def attention(q, kt, v, batch, seq):
    nh, t, _ = q.shape
    tq = kt.shape[3]
    nq = seq // tq
    ah = ATTN_HEADS
    return pl.pallas_call(
        _attn_body,
        grid=(batch, nh // ah, nq),
        in_specs=[pl.BlockSpec((ah, tq, MLA_QK_DIM), lambda b, h, i: (h, b * nq + i, 0)),
                  pl.BlockSpec((ah, nq, MLA_QK_DIM, tq), lambda b, h, i: (h, b, 0, 0)),
                  pl.BlockSpec((ah, seq, MLA_V_DIM), lambda b, h, i: (h, b, 0))],
        out_specs=pl.BlockSpec((tq, ah * MLA_V_DIM), lambda b, h, i: (b * nq + i, h)),
        out_shape=jax.ShapeDtypeStruct((t, nh * MLA_V_DIM), BF16),
        scratch_shapes=[pltpu.VMEM((ah, tq, LANES), F32),
                        pltpu.VMEM((ah, tq, LANES), F32),
                        pltpu.VMEM((ah, tq, MLA_V_DIM), F32)],
        compiler_params=_params("parallel", "parallel", "arbitrary"),
        name="mla_attention",
    )(q, kt, v)


def _ffn_layer(h, norm_w, w_up, conv_w, conv_b, w_down, seq):
    hn = rmsnorm(h, norm_w, BF16)
    act = ffn_up(hn, w_up.astype(BF16), conv_w, conv_b, seq)
    return matmul(act, w_down.astype(BF16), n=w_down.shape[1], res=h, tm=1024, tn=512,
                  name="ffn_down")


def _mla_layer(h, positions, norm_w, w_in, q_norm, kv_norm, w_uq, w_ukv, w_o, batch, seq):
    cc, ss = rope_tables(positions)
    q, kt, v = mla_proj(h, norm_w, w_in, q_norm, kv_norm, w_uq, w_ukv, cc, ss)
    o = attention(q, kt, v, batch, seq)
    return matmul(o, w_o.astype(BF16), n=w_o.shape[1], res=h, tm=1024, tn=1024, name="mla_out")


def _gdn_layer(h, norm_w, w_in, conv_w, a_log, dt_bias, out_norm, w_o, batch, seq):
    t = h.shape[0]
    hn = rmsnorm(h, norm_w, BF16)
    w_in_b = w_in.astype(BF16)
    qkv = gdn_qkv(hn, w_in_b, conv_w, seq)
    z = matmul(hn, w_in_b, n=GDN_VALUE_DIM, b_col0=GDN_CONV_DIM, out_dtype=BF16, tm=1024, tn=512,
               name="gdn_z")
    w_ba = jnp.pad(w_in_b[:, GDN_CONV_DIM + GDN_VALUE_DIM:], ((0, 0), (0, 128 - 2 * GDN_V_HEADS)))
    beta, gc = gdn_gates(hn, w_ba, a_log, dt_bias)
    hk = GDN_QK_HEADS
    gc3 = gc.reshape(t, hk, 2)
    gc_col = gc3.transpose(1, 0, 2)
    gc_row = gc3.transpose(1, 2, 0)
    beta_col = beta.reshape(t, hk, 2).transpose(1, 0, 2)
    tcat = gdn_tinv(qkv, gc_col, gc_row, beta_col, batch, seq)
    o = gdn_scan(qkv, z, tcat, gc_col, gc_row, beta_col, out_norm, batch, seq)
    return matmul(o, w_o.astype(BF16), n=w_o.shape[1], res=h, tm=1024, tn=1024, name="gdn_out")


def kernel(x, positions, mla_norm, mla_w_in, mla_q_norm, mla_kv_norm, mla_w_uq, mla_w_ukv, mla_w_o,
           gdn_norm, gdn_w_in, gdn_conv_w, gdn_a_log, gdn_dt_bias, gdn_out_norm, gdn_w_o, ffn_norm,
           ffn_w_up, ffn_conv_w, ffn_conv_b, ffn_w_down, final_norm):
    batch, seq, d = x.shape
    h = x.reshape(batch * seq, d)
    depth = ffn_norm.shape[0]
    for i in range(depth):
        j = i // 2
        if i % 2 == 0:
            h = _mla_layer(h, positions, mla_norm[j], mla_w_in[j], mla_q_norm[j], mla_kv_norm[j],
                           mla_w_uq[j], mla_w_ukv[j], mla_w_o[j], batch, seq)
        else:
            h = _gdn_layer(h, gdn_norm[j], gdn_w_in[j], gdn_conv_w[j], gdn_a_log[j], gdn_dt_bias[j],
                           gdn_out_norm[j], gdn_w_o[j], batch, seq)
        h = _ffn_layer(h, ffn_norm[i], ffn_w_up[i], ffn_conv_w[i], ffn_conv_b[i], ffn_w_down[i], seq)
    return rmsnorm(h, final_norm, x.dtype).reshape(batch, seq, d)
```

```python
import functools
import math

import jax
import jax.numpy as jnp
from jax import lax
from jax.experimental import pallas as pl
from jax.experimental.pallas import tpu as pltpu

F32 = jnp.float32
BF16 = jnp.bfloat16

RMS_EPS = 1e-6
L2_EPS = 1e-6

MLA_HEADS = 16
MLA_Q_RANK = 512
MLA_KV_RANK = 512
MLA_NOPE_DIM = 128
MLA_ROPE_DIM = 64
MLA_V_DIM = 128
MLA_QK_DIM = MLA_NOPE_DIM + MLA_ROPE_DIM
ROPE_THETA = 10000.0

GDN_QK_HEADS = 16
GDN_V_HEADS = 32
GDN_HEAD_DIM = 128
GDN_KEY_DIM = GDN_QK_HEADS * GDN_HEAD_DIM
GDN_VALUE_DIM = GDN_V_HEADS * GDN_HEAD_DIM
GDN_CONV_DIM = 2 * GDN_KEY_DIM + GDN_VALUE_DIM
GDN_CHUNK = 64

VMEM_LIMIT_BYTES = 56 * 1024 * 1024
SUBLANES = 8
LANES = 128
MXU_WIDTH = 256
NEG_BIG = -1e30
LOG2_E = math.log2(math.e)

ATTN_BLOCK = 512
ATTN_HEADS = 4


def _params(*sem):
    return pltpu.CompilerParams(dimension_semantics=sem, vmem_limit_bytes=VMEM_LIMIT_BYTES)


def _rms(x, w):
    return x * lax.rsqrt(jnp.mean(x * x, axis=-1, keepdims=True) + RMS_EPS) * w


def _sigmoid(x):
    return 1.0 / (1.0 + jnp.exp(-x))


def _rmsnorm_body(x_ref, w_ref, o_ref):
    o_ref[...] = _rms(x_ref[...], w_ref[...]).astype(o_ref.dtype)


def rmsnorm(x, w, out_dtype, tm=512):
    t, d = x.shape
    return pl.pallas_call(
        _rmsnorm_body,
        grid=(t // tm,),
        in_specs=[pl.BlockSpec((tm, d), lambda i: (i, 0)),
                  pl.BlockSpec((1, d), lambda i: (0, 0))],
        out_specs=pl.BlockSpec((tm, d), lambda i: (i, 0)),
        out_shape=jax.ShapeDtypeStruct((t, d), out_dtype),
        compiler_params=_params("parallel"),
        name="rmsnorm",
    )(x, w.reshape(1, d))


def _matmul_body(a_ref, b_ref, o_ref):
    o_ref[...] = jnp.dot(a_ref[...], b_ref[...], preferred_element_type=F32).astype(o_ref.dtype)


def _matmul_res_body(a_ref, b_ref, r_ref, o_ref):
    acc = jnp.dot(a_ref[...], b_ref[...], preferred_element_type=F32)
    o_ref[...] = (r_ref[...] + acc).astype(o_ref.dtype)


def matmul(a, b, layer, *, n, b_col0=0, res=None, out_dtype=F32, tm, tn, name):
    m, k = a.shape
    joff = b_col0 // tn
    in_specs = [pl.BlockSpec((tm, k), lambda i, j: (i, 0)),
                pl.BlockSpec((None, k, tn), lambda i, j: (layer, 0, j + joff))]
    args = [a, b]
    body = _matmul_body
    if res is not None:
        in_specs.append(pl.BlockSpec((tm, tn), lambda i, j: (i, j)))
        args.append(res)
        body = _matmul_res_body
    return pl.pallas_call(
        body,
        grid=(m // tm, n // tn),
        in_specs=in_specs,
        out_specs=pl.BlockSpec((tm, tn), lambda i, j: (i, j)),
        out_shape=jax.ShapeDtypeStruct((m, n), out_dtype),
        compiler_params=_params("parallel", "parallel"),
        name=name,
    )(*args)


def _init_carry(i, j, *carry_refs):
    @pl.when(jnp.logical_and(i == 0, j == 0))
    def _():
        for ref in carry_refs:
            ref[...] = jnp.zeros(ref.shape, F32)


def _conv_head(buf_ref, carry_ref, j, first):
    buf_ref[0:SUBLANES, :] = jnp.where(first, 0.0, carry_ref[j])


def _conv_cols(buf_ref, cw_ref, width, nrows, cs):
    y = cw_ref[width - 1:width, cs] * buf_ref[SUBLANES:SUBLANES + nrows, cs]
    for d in range(1, width):
        y = y + cw_ref[width - 1 - d:width - d, cs] * buf_ref[SUBLANES - d:SUBLANES - d + nrows, cs]
    return y


def _staggered(units, matmul_unit, tail_unit):
    for n, unit in enumerate(units):
        matmul_unit(unit)
        if n >= 1:
            tail_unit(units[n - 1])
    tail_unit(units[-1])


def _ffn_up_body(tiles_per_seq, x_ref, wg_ref, wu_ref, cwg_ref, cwu_ref, bg_ref, bu_ref, o_ref,
                 gbuf, ubuf, gcarry, ucarry):
    i = pl.program_id(0)
    j = pl.program_id(1)
    tm = x_ref.shape[0]
    first = (i % tiles_per_seq) == 0
    _init_carry(i, j, gcarry, ucarry)
    _conv_head(gbuf, gcarry, j, first)
    _conv_head(ubuf, ucarry, j, first)

    x = x_ref[...]
    rows = slice(SUBLANES, SUBLANES + tm)
    cs = slice(0, o_ref.shape[1])
    gbuf[rows, :] = jnp.dot(x, wg_ref[...], preferred_element_type=F32)
    ubuf[rows, :] = jnp.dot(x, wu_ref[...], preferred_element_type=F32)
    yg = _conv_cols(gbuf, cwg_ref, 3, tm, cs) + bg_ref[...]
    yu = _conv_cols(ubuf, cwu_ref, 3, tm, cs) + bu_ref[...]
    o_ref[...] = (yg * _sigmoid(yg) * yu).astype(o_ref.dtype)
    gcarry[j] = gbuf[tm:tm + SUBLANES, :]
    ucarry[j] = ubuf[tm:tm + SUBLANES, :]


def ffn_up(hn, w_up, layer, conv_w, conv_b, seq, tm=1024, tf=512):
    t, d = hn.shape
    f = w_up.shape[2] // 2
    nj = f // tf
    return pl.pallas_call(
        functools.partial(_ffn_up_body, seq // tm),
        grid=(t // tm, nj),
        in_specs=[pl.BlockSpec((tm, d), lambda i, j: (i, 0)),
                  pl.BlockSpec((None, d, tf), lambda i, j: (layer, 0, j)),
                  pl.BlockSpec((None, d, tf), lambda i, j: (layer, 0, j + nj)),
                  pl.BlockSpec((3, tf), lambda i, j: (0, j)),
                  pl.BlockSpec((3, tf), lambda i, j: (0, j + nj)),
                  pl.BlockSpec((1, tf), lambda i, j: (0, j)),
                  pl.BlockSpec((1, tf), lambda i, j: (0, j + nj))],
        out_specs=pl.BlockSpec((tm, tf), lambda i, j: (i, j)),
        out_shape=jax.ShapeDtypeStruct((t, f), BF16),
        scratch_shapes=[pltpu.VMEM((tm + SUBLANES, tf), F32),
                        pltpu.VMEM((tm + SUBLANES, tf), F32),
                        pltpu.VMEM((nj, SUBLANES, tf), F32),
                        pltpu.VMEM((nj, SUBLANES, tf), F32)],
        compiler_params=_params("arbitrary", "arbitrary"),
        name="ffn_up",
    )(hn, w_up, w_up, conv_w, conv_w, conv_b.reshape(1, -1), conv_b.reshape(1, -1))


GDN_QKV_SUBTILE = 512


def _gdn_qkv_body(tiles_per_seq, n_q_tiles, n_qk_tiles, x_ref, w_ref, cw_ref, o_ref, buf, carry):
    i = pl.program_id(0)
    j = pl.program_id(1)
    tm = x_ref.shape[0]
    first = (i % tiles_per_seq) == 0
    _init_carry(i, j, carry)
    _conv_head(buf, carry, j, first)
    is_qk = j < n_qk_tiles
    scale = jnp.where(j < n_q_tiles, GDN_HEAD_DIM ** -0.5, 1.0).astype(F32)
    dh = GDN_HEAD_DIM
    x = x_ref[...]

    def matmul_unit(cs):
        buf[SUBLANES:SUBLANES + tm, cs] = jnp.dot(x, w_ref[:, cs], preferred_element_type=F32)

    def tail_unit(cs):
        y = _conv_cols(buf, cw_ref, 4, tm, cs)
        y = y * _sigmoid(y)
        for g in range(GDN_QKV_SUBTILE // dh):
            blk = y[:, g * dh:(g + 1) * dh]
            nrm = lax.rsqrt(jnp.sum(blk * blk, axis=-1, keepdims=True) + L2_EPS) * scale
            nrm = jnp.where(is_qk, nrm, 1.0)
            c0 = cs.start + g * dh
            o_ref[:, c0:c0 + dh] = (blk * nrm).astype(o_ref.dtype)

    units = [slice(c0, c0 + GDN_QKV_SUBTILE) for c0 in range(0, o_ref.shape[1], GDN_QKV_SUBTILE)]
    _staggered(units, matmul_unit, tail_unit)
    carry[j] = buf[tm:tm + SUBLANES, :]


def gdn_qkv(hn, w_in, layer, conv_w, seq, tm=1024, tn=2048):
    t, d = hn.shape
    nj = GDN_CONV_DIM // tn
    return pl.pallas_call(
        functools.partial(_gdn_qkv_body, seq // tm, GDN_KEY_DIM // tn, 2 * GDN_KEY_DIM // tn),
        grid=(t // tm, nj),
        in_specs=[pl.BlockSpec((tm, d), lambda i, j: (i, 0)),
                  pl.BlockSpec((None, d, tn), lambda i, j: (layer, 0, j)),
                  pl.BlockSpec((4, tn), lambda i, j: (0, j))],
        out_specs=pl.BlockSpec((tm, tn), lambda i, j: (i, j)),
        out_shape=jax.ShapeDtypeStruct((t, GDN_CONV_DIM), BF16),
        scratch_shapes=[pltpu.VMEM((tm + SUBLANES, tn), F32),
                        pltpu.VMEM((nj, SUBLANES, tn), F32)],
        compiler_params=_params("arbitrary", "arbitrary"),
        name="gdn_qkv",
    )(hn, w_in, conv_w)


GDN_HEAD_GROUP = 8


def _gdn_gate_body(x_ref, w_ref, alog_ref, dtb_ref, beta_ref, gc_ref, gct_ref):
    tm = x_ref.shape[0]
    nh = GDN_V_HEADS
    hg = GDN_HEAD_GROUP
    ba = jnp.dot(x_ref[...], w_ref[...], preferred_element_type=F32)
    b = ba[:, 0:nh]
    a = ba[:, nh:2 * nh] + dtb_ref[...]
    beta = _sigmoid(b)
    softplus = jnp.maximum(a, 0.0) + jnp.log(1.0 + jnp.exp(-jnp.abs(a)))
    g = -jnp.exp(alog_ref[...]) * softplus
    row = lax.broadcasted_iota(jnp.int32, (tm, tm), 0)
    col = lax.broadcasted_iota(jnp.int32, (tm, tm), 1)
    same_chunk = (row // GDN_CHUNK) == (col // GDN_CHUNK)
    tri = jnp.where(jnp.logical_and(same_chunk, col <= row), 1.0, 0.0).astype(F32)
    gc = jnp.dot(tri, g, preferred_element_type=F32, precision=lax.Precision.HIGHEST)
    gc_t = jnp.concatenate([gc, jnp.zeros((tm, LANES - nh), F32)], axis=1).T
    for i in range(nh // hg):
        beta_ref[i] = beta[:, i * hg:(i + 1) * hg]
        gc_ref[i] = gc[:, i * hg:(i + 1) * hg]
        gct_ref[i] = gc_t[i * hg:(i + 1) * hg, :]


def gdn_gates(hn, w_ba, a_log, dt_bias, tm=256):
    t, d = hn.shape
    nh = GDN_V_HEADS
    hg = GDN_HEAD_GROUP
    ng = nh // hg
    return pl.pallas_call(
        _gdn_gate_body,
        grid=(t // tm,),
        in_specs=[pl.BlockSpec((tm, d), lambda i: (i, 0)),
                  pl.BlockSpec((d, LANES), lambda i: (0, 0)),
                  pl.BlockSpec((1, nh), lambda i: (0, 0)),
                  pl.BlockSpec((1, nh), lambda i: (0, 0))],
        out_specs=[pl.BlockSpec((ng, tm, hg), lambda i: (0, i, 0)),
                   pl.BlockSpec((ng, tm, hg), lambda i: (0, i, 0)),
                   pl.BlockSpec((ng, hg, tm), lambda i: (0, 0, i))],
        out_shape=[jax.ShapeDtypeStruct((ng, t, hg), F32),
                   jax.ShapeDtypeStruct((ng, t, hg), F32),
                   jax.ShapeDtypeStruct((ng, hg, t), F32)],
        compiler_params=_params("parallel"),
        name="gdn_gates",
    )(hn, w_ba, a_log.reshape(1, nh), dt_bias.reshape(1, nh))


def _dot_t(a, b):
    return lax.dot_general(a, b, (((1,), (1,)), ((), ())), preferred_element_type=F32)


GDN_GROUP = 4
GDN_GT = GDN_GROUP * GDN_CHUNK


def _group_masks():
    row = lax.broadcasted_iota(jnp.int32, (GDN_GT, GDN_GT), 0)
    col = lax.broadcasted_iota(jnp.int32, (GDN_GT, GDN_GT), 1)
    same = (row // GDN_CHUNK) == (col // GDN_CHUNK)
    return same, jnp.logical_and(same, col <= row), jnp.logical_and(same, col < row)


def _fold(x):
    c = GDN_CHUNK
    out = x[0:c]
    for i in range(1, GDN_GROUP):
        out = out + x[i * c:(i + 1) * c]
    return out


def _unfold(xcat, same):
    return jnp.where(same, jnp.concatenate([xcat] * GDN_GROUP, axis=0), jnp.zeros((), xcat.dtype))


def _gdn_tinv_body(k_ref, gc_ref, gct_ref, beta_ref, t_ref):
    c = GDN_CHUNK
    tt = k_ref.shape[0]
    same, tril, strict = _group_masks()
    eye_cat = jnp.where(
        jnp.bitwise_and(lax.broadcasted_iota(jnp.int32, (c, GDN_GT), 1), c - 1)
        == lax.broadcasted_iota(jnp.int32, (c, GDN_GT), 0), 1.0, 0.0).astype(F32)
    dh = GDN_HEAD_DIM
    n_qk = k_ref.shape[1] // dh
    chains = [(g, hv) for g in range(tt // GDN_GT) for hv in range(2 * n_qk)]
    grams = {}
    for g in range(tt // GDN_GT):
        for a in range(n_qk):
            k4 = k_ref[g * GDN_GT:(g + 1) * GDN_GT, a * dh:(a + 1) * dh]
            grams[g, a] = _dot_t(k4, k4)
    pws, pcats = [], []
    for g, j in chains:
        r = slice(g * GDN_GT, (g + 1) * GDN_GT)
        gcol = gc_ref[r, j:j + 1]
        grow = gct_ref[j:j + 1, r]
        bcol = beta_ref[r, j:j + 1]
        dec = jnp.exp(jnp.where(tril, gcol - grow, NEG_BIG))
        afull = jnp.where(strict, -(bcol * grams[g, j // 2]) * dec, 0.0)
        acat = _fold(afull)
        pcats.append(eye_cat + acat)
        pws.append(jnp.dot(acat.astype(BF16), afull.astype(BF16), preferred_element_type=F32))
    for _ in range(GDN_GROUP):
        for n in range(len(chains)):
            rhs = _unfold(pws[n].astype(BF16), same)
            lhs = jnp.concatenate([pws[n], pcats[n]], axis=0).astype(BF16)
            out = jnp.dot(lhs, rhs, preferred_element_type=F32)
            pws[n] = out[0:c]
            pcats[n] = pcats[n] + out[c:2 * c]
    for n, (g, j) in enumerate(chains):
        rhs = _unfold(pws[n].astype(BF16), same)
        tfin = pcats[n] + jnp.dot(pcats[n].astype(BF16), rhs, preferred_element_type=F32)
        t_ref[j, g] = tfin.astype(t_ref.dtype)


def gdn_tinv(qkv, gc, gct, beta, batch, seq, tt=GDN_GT):
    nt = seq // tt
    hg = GDN_HEAD_GROUP
    kw = hg // 2 * GDN_HEAD_DIM
    k_blocks = GDN_KEY_DIM // kw
    t = qkv.shape[0]
    ng = tt // GDN_GT
    return pl.pallas_call(
        _gdn_tinv_body,
        grid=(batch, GDN_V_HEADS // hg, nt),
        in_specs=[pl.BlockSpec((tt, kw), lambda b, h, n: (b * nt + n, k_blocks + h)),
                  pl.BlockSpec((None, tt, hg), lambda b, h, n: (h, b * nt + n, 0)),
                  pl.BlockSpec((None, hg, tt), lambda b, h, n: (h, 0, b * nt + n)),
                  pl.BlockSpec((None, tt, hg), lambda b, h, n: (h, b * nt + n, 0))],
        out_specs=pl.BlockSpec((hg, ng, GDN_CHUNK, GDN_GT), lambda b, h, n: (h, b * nt + n, 0, 0)),
        out_shape=jax.ShapeDtypeStruct((GDN_V_HEADS, t // GDN_GT, GDN_CHUNK, GDN_GT), BF16),
        compiler_params=_params("parallel", "parallel", "parallel"),
        name="gdn_tinv",
    )(qkv, gc, gct, beta)


def _gdn_scan_body(q_ref, k_ref, v_ref, z_ref, t_ref, gc_ref, gct_ref, beta_ref, nw_ref, o_ref, s_ref):
    c = GDN_CHUNK
    dh = GDN_HEAD_DIM
    n_qk = q_ref.shape[1] // dh

    @pl.when(pl.program_id(2) == 0)
    def _():
        s_ref[...] = jnp.zeros(s_ref.shape, F32)

    same, tril, _ = _group_masks()
    heads = [(a, j) for a in range(n_qk) for j in range(2)]
    qks, kfs, qfs = [], [], []
    for a in range(n_qk):
        q4 = q_ref[:, a * dh:(a + 1) * dh]
        k4 = k_ref[:, a * dh:(a + 1) * dh]
        qks.append(_dot_t(q4, k4))
        qfs.append(q4.astype(F32))
        kfs.append(k4.astype(F32))
    us, wqs, attns, k_decs, gts, states = [], [], [], [], [], []
    for a, j in heads:
        hv = 2 * a + j
        cols = slice(hv * dh, (hv + 1) * dh)
        gcol = gc_ref[:, hv:hv + 1]
        grow = gct_ref[hv:hv + 1, :]
        bcol = beta_ref[:, hv:hv + 1]
        dec = jnp.exp(jnp.where(tril, gcol - grow, NEG_BIG))
        attns.append((qks[a] * dec).astype(BF16))
        tfull = _unfold(t_ref[hv, 0], same)
        egc = jnp.exp(gcol)
        vb = (v_ref[:, cols].astype(F32) * bcol).astype(BF16)
        kb = (kfs[a] * (bcol * egc)).astype(BF16)
        uw = jnp.dot(tfull, jnp.concatenate([vb, kb], axis=1), preferred_element_type=F32)
        us.append(uw[:, 0:dh])
        w = uw[:, dh:2 * dh].astype(BF16)
        q_dec = (qfs[a] * egc).astype(BF16)
        wqs.append([jnp.concatenate([w[i * c:(i + 1) * c], q_dec[i * c:(i + 1) * c]], axis=0)
                    for i in range(GDN_GROUP)])
        glasts = [gcol[(i + 1) * c - 1:(i + 1) * c, :] for i in range(GDN_GROUP)]
        glast = jnp.concatenate([jnp.broadcast_to(gl, (c, 1)) for gl in glasts], axis=0)
        k_decs.append((kfs[a] * jnp.exp(glast - gcol)).astype(BF16))
        gts.append([jnp.exp(gl) for gl in glasts])
        states.append(s_ref[hv])
    v_new = [[] for _ in heads]
    o_inter = [[] for _ in heads]
    for i in range(GDN_GROUP):
        rc = slice(i * c, (i + 1) * c)
        vnbs = []
        for n in range(len(heads)):
            ws = jnp.dot(wqs[n][i], states[n].astype(BF16), preferred_element_type=F32)
            vnbs.append((us[n][rc] - ws[0:c]).astype(BF16))
            o_inter[n].append(ws[c:2 * c])
        for n in range(len(heads)):
            states[n] = states[n] * gts[n][i] + lax.dot_general(
                k_decs[n][rc], vnbs[n], (((0,), (0,)), ((), ())), preferred_element_type=F32)
            v_new[n].append(vnbs[n])
    for n, (a, j) in enumerate(heads):
        hv = 2 * a + j
        cols = slice(hv * dh, (hv + 1) * dh)
        s_ref[hv] = states[n]
        o = jnp.concatenate(o_inter[n], axis=0) + jnp.dot(
            attns[n], jnp.concatenate(v_new[n], axis=0), preferred_element_type=F32)
        zf = z_ref[:, cols].astype(F32)
        o_ref[:, cols] = (_rms(o, nw_ref[...]) * (zf * _sigmoid(zf))).astype(o_ref.dtype)


def gdn_scan(qkv, z, tcat, gc, gct, beta, out_norm, batch, seq):
    t = qkv.shape[0]
    tt = GDN_GT
    nt = seq // tt
    dh = GDN_HEAD_DIM
    hv = GDN_HEAD_GROUP
    kw = hv // 2 * dh
    qk_blocks = GDN_KEY_DIM // kw
    return pl.pallas_call(
        _gdn_scan_body,
        grid=(batch, GDN_V_HEADS // hv, nt),
        in_specs=[pl.BlockSpec((tt, kw), lambda b, h, n: (b * nt + n, h)),
                  pl.BlockSpec((tt, kw), lambda b, h, n: (b * nt + n, qk_blocks + h)),
                  pl.BlockSpec((tt, hv * dh), lambda b, h, n: (b * nt + n, qk_blocks + h)),
                  pl.BlockSpec((tt, hv * dh), lambda b, h, n: (b * nt + n, h)),
                  pl.BlockSpec((hv, 1, GDN_CHUNK, GDN_GT), lambda b, h, n: (h, b * nt + n, 0, 0)),
                  pl.BlockSpec((None, tt, hv), lambda b, h, n: (h, b * nt + n, 0)),
                  pl.BlockSpec((None, hv, tt), lambda b, h, n: (h, 0, b * nt + n)),
                  pl.BlockSpec((None, tt, hv), lambda b, h, n: (h, b * nt + n, 0)),
                  pl.BlockSpec((1, dh), lambda b, h, n: (0, 0))],
        out_specs=pl.BlockSpec((tt, hv * dh), lambda b, h, n: (b * nt + n, h)),
        out_shape=jax.ShapeDtypeStruct((t, GDN_VALUE_DIM), BF16),
        scratch_shapes=[pltpu.VMEM((hv, dh, dh), F32)],
        compiler_params=_params("parallel", "parallel", "arbitrary"),
        name="gdn_scan",
    )(qkv, qkv, qkv, z, tcat, gc, gct, beta, out_norm.reshape(1, dh))


def _rope_table_body(pos_ref, freq_ref, sign_ref, cc_ref, ss_ref):
    ang = pos_ref[...].astype(F32) * freq_ref[...]
    cc_ref[...] = jnp.cos(ang)
    ss_ref[...] = jnp.sin(ang) * sign_ref[...]


def rope_tables(positions, tm=2048):
    t = positions.size
    half = MLA_ROPE_DIM // 2
    inv_freq = ROPE_THETA ** (-jnp.arange(0, MLA_ROPE_DIM, 2, dtype=F32) / MLA_ROPE_DIM)
    freq = jnp.concatenate([inv_freq, inv_freq]).reshape(1, MLA_ROPE_DIM)
    sign = jnp.concatenate([-jnp.ones((half,), F32), jnp.ones((half,), F32)]).reshape(1, MLA_ROPE_DIM)
    tm = min(tm, t)
    return pl.pallas_call(
        _rope_table_body,
        grid=(t // tm,),
        in_specs=[pl.BlockSpec((tm, 1), lambda i: (i, 0)),
                  pl.BlockSpec((1, MLA_ROPE_DIM), lambda i: (0, 0)),
                  pl.BlockSpec((1, MLA_ROPE_DIM), lambda i: (0, 0))],
        out_specs=[pl.BlockSpec((tm, MLA_ROPE_DIM), lambda i: (i, 0)),
                   pl.BlockSpec((tm, MLA_ROPE_DIM), lambda i: (i, 0))],
        out_shape=[jax.ShapeDtypeStruct((t, MLA_ROPE_DIM), F32),
                   jax.ShapeDtypeStruct((t, MLA_ROPE_DIM), F32)],
        compiler_params=_params("parallel"),
        name="rope_tables",
    )(positions.reshape(t, 1), freq, sign)


def _mla_proj_body(x_ref, nw_ref, win_ref, qnw_ref, kvnw_ref, wq_ref, wkv_ref, cc_ref, ss_ref,
                   q_ref, kt_ref, v_ref):
    nope, rope = MLA_NOPE_DIM, MLA_ROPE_DIM
    hw = nope + 2 * rope
    xn = _rms(x_ref[...], nw_ref[...]).astype(BF16)
    c = jnp.dot(xn, win_ref[...], preferred_element_type=F32)
    cq = _rms(c[:, 0:MLA_Q_RANK], qnw_ref[...]).astype(BF16)
    ckv = _rms(c[:, MLA_Q_RANK:MLA_Q_RANK + MLA_KV_RANK], kvnw_ref[...]).astype(BF16)
    cc = cc_ref[...]
    ss = ss_ref[...]
    r0 = MLA_Q_RANK + MLA_KV_RANK
    kr = c[:, r0:r0 + rope] * cc + c[:, r0 + rope:r0 + 2 * rope] * ss
    scale = MLA_QK_DIM ** -0.5 * LOG2_E
    kr_pad = jnp.concatenate([kr, jnp.zeros_like(kr)], axis=1)
    for h in range(MLA_HEADS):
        aq = jnp.dot(cq, wq_ref[:, h * hw:(h + 1) * hw], preferred_element_type=F32)
        q_ref[h, :, 0:nope] = (aq[:, 0:nope] * scale).astype(BF16)
        qr = aq[:, nope:nope + rope] * cc + aq[:, nope + rope:nope + 2 * rope] * ss
        q_ref[h, :, nope:nope + rope] = (qr * scale).astype(BF16)
        akv = jnp.dot(ckv, wkv_ref[:, h * 256:(h + 1) * 256], preferred_element_type=F32)
        k_t = jnp.concatenate([akv[:, 0:nope], kr_pad], axis=1).T
        kt_ref[h, 0] = k_t[0:MLA_QK_DIM, :].astype(BF16)
        v_ref[h] = akv[:, nope:nope + MLA_V_DIM].astype(BF16)


def _swap_halves(w):
    half = w.shape[-1] // 2
    return jnp.concatenate([w[..., half:], w[..., :half]], axis=-1)


def mla_proj(x, norm_w, w_in, q_norm, kv_norm, w_uq, w_ukv, cc, ss, tm=ATTN_BLOCK):
    t, d = x.shape
    nh, nope, rope = MLA_HEADS, MLA_NOPE_DIM, MLA_ROPE_DIM
    r0 = MLA_Q_RANK + MLA_KV_RANK
    w_kr = w_in[:, r0:r0 + rope]
    w_in_ext = jnp.concatenate([w_in[:, :r0], w_kr, _swap_halves(w_kr)], axis=1).astype(BF16)
    wq = w_uq.reshape(MLA_Q_RANK, nh, MLA_QK_DIM)
    wq_ext = jnp.concatenate([wq, _swap_halves(wq[..., nope:])], axis=-1)
    hw = nope + 2 * rope
    wq_ext = wq_ext.reshape(MLA_Q_RANK, nh * hw).astype(BF16)
    wkv = w_ukv.astype(BF16)
    const = lambda i: (0, 0)
    resident = dict(pipeline_mode=pl.Buffered(1))
    return pl.pallas_call(
        _mla_proj_body,
        grid=(t // tm,),
        in_specs=[pl.BlockSpec((tm, d), lambda i: (i, 0)),
                  pl.BlockSpec((1, d), const),
                  pl.BlockSpec(w_in_ext.shape, const, **resident),
                  pl.BlockSpec((1, MLA_Q_RANK), const),
                  pl.BlockSpec((1, MLA_KV_RANK), const),
                  pl.BlockSpec(wq_ext.shape, const, **resident),
                  pl.BlockSpec(wkv.shape, const, **resident),
                  pl.BlockSpec((tm, rope), lambda i: (i, 0)),
                  pl.BlockSpec((tm, rope), lambda i: (i, 0))],
        out_specs=[pl.BlockSpec((nh, tm, MLA_QK_DIM), lambda i: (0, i, 0)),
                   pl.BlockSpec((nh, 1, MLA_QK_DIM, tm), lambda i: (0, i, 0, 0)),
                   pl.BlockSpec((nh, tm, MLA_V_DIM), lambda i: (0, i, 0))],
        out_shape=[jax.ShapeDtypeStruct((nh, t, MLA_QK_DIM), BF16),
                   jax.ShapeDtypeStruct((nh, t // tm, MLA_QK_DIM, tm), BF16),
                   jax.ShapeDtypeStruct((nh, t, MLA_V_DIM), BF16)],
        compiler_params=_params("parallel"),
        name="mla_proj",
    )(x, norm_w.reshape(1, d), w_in_ext, q_norm.reshape(1, -1), kv_norm.reshape(1, -1),
      wq_ext, wkv, cc, ss)


def _attn_body(q_ref, kt_ref, v_ref, o_ref, m_ref, l_ref, acc_ref):
    nhead, tq, _ = q_ref.shape
    tk = kt_ref.shape[3]
    nlt = tk // LANES
    qi = pl.program_id(2)
    m_ref[...] = jnp.full(m_ref.shape, NEG_BIG, F32)
    l_ref[...] = jnp.zeros(l_ref.shape, F32)
    acc_ref[...] = jnp.zeros(acc_ref.shape, F32)

    def scores(hd, kb):
        return jnp.dot(q_ref[hd], kt_ref[hd, kb], preferred_element_type=F32)

    def update(hd, kb, s, masked):
        start = pl.multiple_of(kb * tk, tk)
        if masked:
            row = lax.broadcasted_iota(jnp.int32, (tq, tk), 0)
            col = lax.broadcasted_iota(jnp.int32, (tq, tk), 1)
            s = jnp.where(col <= row, s, NEG_BIG)
        m_prev = m_ref[hd]
        m_new = jnp.maximum(m_prev, jnp.max(s, axis=-1, keepdims=True))
        alpha = jnp.exp2(m_prev - m_new)
        p = jnp.exp2(s - jnp.concatenate([m_new] * nlt, axis=1))
        psum = p[:, 0:LANES]
        for i in range(1, nlt):
            psum = psum + p[:, i * LANES:(i + 1) * LANES]
        l_ref[hd] = alpha * l_ref[hd] + psum
        acc_ref[hd] = alpha * acc_ref[hd] + jnp.dot(
            p.astype(BF16), v_ref[hd, pl.ds(start, tk), :], preferred_element_type=F32)
        m_ref[hd] = m_new

    def block(kb, masked):
        ss = [scores(hd, kb) for hd in range(nhead)]
        for hd in range(nhead):
            update(hd, kb, ss[hd], masked)

    def full_step(kb, carry):
        block(kb, False)
        return carry

    lax.fori_loop(0, qi, full_step, 0)
    block(qi, True)
    for hd in range(nhead):
        l = jnp.sum(l_ref[hd], axis=-1, keepdims=True)
        o_ref[:, hd * MLA_V_DIM:(hd + 1) * MLA_V_DIM] = (acc_ref[hd] / l).astype(o_ref.dtype)


def attention(q, kt, v, batch, seq):
    nh, t, _ = q.shape
    tq = kt.shape[3]
    nq = seq // tq
    ah = ATTN_HEADS
    return pl.pallas_call(
        _attn_body,
        grid=(batch, nh // ah, nq),
        in_specs=[pl.BlockSpec((ah, tq, MLA_QK_DIM), lambda b, h, i: (h, b * nq + i, 0)),
                  pl.BlockSpec((ah, nq, MLA_QK_DIM, tq), lambda b, h, i: (h, b, 0, 0)),
                  pl.BlockSpec((ah, seq, MLA_V_DIM), lambda b, h, i: (h, b, 0))],
        out_specs=pl.BlockSpec((tq, ah * MLA_V_DIM), lambda b, h, i: (b * nq + i, h)),
        out_shape=jax.ShapeDtypeStruct((t, nh * MLA_V_DIM), BF16),
        scratch_shapes=[pltpu.VMEM((ah, tq, LANES), F32),
                        pltpu.VMEM((ah, tq, LANES), F32),
                        pltpu.VMEM((ah, tq, MLA_V_DIM), F32)],
        compiler_params=_params("parallel", "parallel", "arbitrary"),
        name="mla_attention",
    )(q, kt, v)


def _ffn_layer(h, norm_w, w_up_b, layer, conv_w, conv_b, w_down_b, seq):
    hn = rmsnorm(h, norm_w, BF16)
    act = ffn_up(hn, w_up_b, layer, conv_w, conv_b, seq)
    return matmul(act, w_down_b, layer, n=w_down_b.shape[2], res=h, tm=1024, tn=512, name="ffn_down")


def _mla_layer(h, positions, norm_w, w_in, q_norm, kv_norm, w_uq, w_ukv, w_o_b, layer, batch, seq):
    cc, ss = rope_tables(positions)
    q, kt, v = mla_proj(h, norm_w, w_in, q_norm, kv_norm, w_uq, w_ukv, cc, ss)
    o = attention(q, kt, v, batch, seq)
    return matmul(o, w_o_b, layer, n=w_o_b.shape[2], res=h, tm=1024, tn=1024, name="mla_out")


def _gdn_layer(h, norm_w, w_in_b, layer, conv_w, a_log, dt_bias, out_norm, w_o_b, batch, seq):
    hn = rmsnorm(h, norm_w, BF16)
    qkv = gdn_qkv(hn, w_in_b, layer, conv_w, seq)
    z = matmul(hn, w_in_b, layer, n=GDN_VALUE_DIM, b_col0=GDN_CONV_DIM, out_dtype=BF16, tm=1024, tn=512,
               name="gdn_z")
    w_ba = jnp.pad(w_in_b[layer, :, GDN_CONV_DIM + GDN_VALUE_DIM:], ((0, 0), (0, LANES - 2 * GDN_V_HEADS)))
    beta, gc, gct = gdn_gates(hn, w_ba, a_log, dt_bias)
    tcat = gdn_tinv(qkv, gc, gct, beta, batch, seq)
    o = gdn_scan(qkv, z, tcat, gc, gct, beta, out_norm, batch, seq)
    return matmul(o, w_o_b, layer, n=w_o_b.shape[2], res=h, tm=1024, tn=1024, name="gdn_out")


def kernel(x, positions, mla_norm, mla_w_in, mla_q_norm, mla_kv_norm, mla_w_uq, mla_w_ukv, mla_w_o,
           gdn_norm, gdn_w_in, gdn_conv_w, gdn_a_log, gdn_dt_bias, gdn_out_norm, gdn_w_o, ffn_norm,
           ffn_w_up, ffn_conv_w, ffn_conv_b, ffn_w_down, final_norm):
    batch, seq, d = x.shape
    h = x.reshape(batch * seq, d)
    depth = ffn_norm.shape[0]
    mla_w_o_b = mla_w_o.astype(BF16)
    gdn_w_in_b = gdn_w_in.astype(BF16)
    gdn_w_o_b = gdn_w_o.astype(BF16)
    ffn_w_up_b = ffn_w_up.astype(BF16)
    ffn_w_down_b = ffn_w_down.astype(BF16)
    for i in range(depth):
        j = i // 2
        if i % 2 == 0:
            h = _mla_layer(h, positions, mla_norm[j], mla_w_in[j], mla_q_norm[j], mla_kv_norm[j],
                           mla_w_uq[j], mla_w_ukv[j], mla_w_o_b, j, batch, seq)
        else:
            h = _gdn_layer(h, gdn_norm[j], gdn_w_in_b, j, gdn_conv_w[j], gdn_a_log[j], gdn_dt_bias[j],
                           gdn_out_norm[j], gdn_w_o_b, batch, seq)
        h = _ffn_layer(h, ffn_norm[i], ffn_w_up_b, i, ffn_conv_w[i], ffn_conv_b[i], ffn_w_down_b, seq)
    return rmsnorm(h, final_norm, x.dtype).reshape(batch, seq, d)
```
